```python
import jax
import jax.numpy as jnp
from jax import lax
import numpy as np

D_MODEL = 1024
BATCH = 1
SEQ = 16384
DEPTH = 2
DEC_BATCH = 32
DEC_SEQ = 16
PAST_LEN = 4096

CHUNK = 64
N_HEADS = 8
QK_NOPE = 128
QK_ROPE = 64
V_DIM = 128
Q_LORA = 384
KV_LORA = 512
ATTN_WIDTH = N_HEADS * V_DIM
GM_CHUNK = 128
GM_WIDTH = 1024
GM_GROUPS = 8
GM_GROUP_DIM = GM_WIDTH // GM_GROUPS
D_FF = 2816
PLE_DIM = 256
ROPE_THETA = 10000.0
Q_BLOCK = 128
EPS = 1e-6
SCALE = (QK_NOPE + QK_ROPE) ** -0.5
SPLITS = [Q_LORA,
          Q_LORA + KV_LORA,
          Q_LORA + KV_LORA + QK_ROPE,
          Q_LORA + KV_LORA + QK_ROPE + GM_WIDTH,
          Q_LORA + KV_LORA + QK_ROPE + 2 * GM_WIDTH,
          Q_LORA + KV_LORA + QK_ROPE + 2 * GM_WIDTH + D_MODEL]
D_IN = Q_LORA + KV_LORA + QK_ROPE + 2 * GM_WIDTH + 2 * D_MODEL

kernel_name = 'hybrid_mla_gmlp_streaming_step'


def rmsnorm(x, g):
    xf = x.astype(jnp.float32)
    y = xf * lax.rsqrt(jnp.mean(xf * xf, axis=-1, keepdims=True) + EPS)
    return (y * g.astype(jnp.float32)).astype(x.dtype)


def rope(x, pos):
    half = QK_ROPE // 2
    inv = ROPE_THETA ** (-jnp.arange(half, dtype=jnp.float32) / half)
    ang = pos.astype(jnp.float32)[:, None] * inv[None, :]
    cos = jnp.cos(ang)[None, :, None, :].astype(x.dtype)
    sin = jnp.sin(ang)[None, :, None, :].astype(x.dtype)
    x1, x2 = x[..., :half], x[..., half:]
    return jnp.concatenate([x1 * cos - x2 * sin, x1 * sin + x2 * cos], axis=-1)


def swiglu(x, w_gate, w_up, w_down):
    return (jax.nn.silu(x @ w_gate) * (x @ w_up)) @ w_down


def prompt_attention(q_nope, q_rope, k_nope, k_rope, v):
    B, S = q_nope.shape[:2]
    nb = S // Q_BLOCK
    qn = q_nope.reshape(B, nb, Q_BLOCK, N_HEADS, QK_NOPE).transpose(1, 0, 2, 3, 4)
    qr = q_rope.reshape(B, nb, Q_BLOCK, N_HEADS, QK_ROPE).transpose(1, 0, 2, 3, 4)
    key_chunk = jnp.arange(S) // CHUNK
    neg = jnp.finfo(jnp.float32).min

    def block(args):
        qn_b, qr_b, start = args
        s = (jnp.einsum('bqhd,bkhd->bhqk', qn_b, k_nope)
             + jnp.einsum('bqhr,bkr->bhqk', qr_b, k_rope)).astype(jnp.float32) * SCALE
        q_chunk = (start + jnp.arange(Q_BLOCK)) // CHUNK
        mask = key_chunk[None, :] <= q_chunk[:, None]
        s = jnp.where(mask[None, None], s, neg)
        p = jax.nn.softmax(s, axis=-1).astype(v.dtype)
        return jnp.einsum('bhqk,bkhd->bqhd', p, v)

    out = lax.map(block, (qn, qr, jnp.arange(nb, dtype=jnp.int32) * Q_BLOCK))
    return out.transpose(1, 0, 2, 3, 4).reshape(B, S, ATTN_WIDTH)


def sample_attention(q_nope, q_rope, k_nope, k_rope, v):
    B, Q = q_nope.shape[:2]
    s = (jnp.einsum('bqhd,bkhd->bhqk', q_nope, k_nope)
         + jnp.einsum('bqhr,bkr->bhqk', q_rope, k_rope)).astype(jnp.float32) * SCALE
    p = jax.nn.softmax(s, axis=-1).astype(v.dtype)
    return jnp.einsum('bhqk,bkhd->bqhd', p, v).reshape(B, Q, ATTN_WIDTH)


def spatial_gating(u, v, w_s, b_s):
    B, S, _ = v.shape
    L = GM_CHUNK if S >= GM_CHUNK else S
    nc = S // L
    tri = jnp.tril(jnp.ones((L, L), dtype=bool))
    w = jnp.where(tri[None], w_s[:, :L, :L], jnp.zeros((), w_s.dtype))
    vc = v.reshape(B, nc, L, GM_GROUPS, GM_GROUP_DIM)
    mixed = jnp.einsum('gts,bcsgd->bctgd', w, vc) + b_s[:, :L].T[None, None, :, :, None]
    return u * mixed.reshape(B, S, GM_WIDTH)


def run_layer(h, p_l, pos, past_c, past_kr, wl):
    (f1_norm, f1_wg, f1_wu, f1_wd, mix_norm, w_in, q_a_norm, w_uq, q_nope_norm,
     q_rope_norm, kv_a_norm, k_rope_norm, w_uk, w_uv, k_nope_norm, gm_v_norm,
     gm_w_s, gm_b_s, w_o, f2_norm, f2_wg, f2_wu, f2_wd, ple_norm, ple_w_gate,
     ple_w_proj) = wl
    B, S, _ = h.shape
    h = h + 0.5 * swiglu(rmsnorm(h, f1_norm), f1_wg, f1_wu, f1_wd)
    n = rmsnorm(h, mix_norm)
    z = n @ w_in
    q_lat, kv_lat, kr_raw, u, v, g_a, g_b = jnp.split(z, SPLITS, axis=-1)
    q = (rmsnorm(q_lat, q_a_norm) @ w_uq).reshape(B, S, N_HEADS, QK_NOPE + QK_ROPE)
    q_nope = rmsnorm(q[..., :QK_NOPE], q_nope_norm)
    q_rope = rope(rmsnorm(q[..., QK_NOPE:], q_rope_norm), pos)
    c = rmsnorm(kv_lat, kv_a_norm)
    kr = rope(rmsnorm(kr_raw, k_rope_norm)[:, :, None, :], pos)[:, :, 0, :]
    if past_c is None:
        c_all, kr_all = c, kr
    else:
        c_all = jnp.concatenate([past_c, c], axis=1)
        kr_all = jnp.concatenate([past_kr, kr], axis=1)
    k_nope = rmsnorm(jnp.einsum('bsc,chd->bshd', c_all, w_uk), k_nope_norm)
    v_a = jnp.einsum('bsc,chd->bshd', c_all, w_uv)
    if past_c is None:
        o_a = prompt_attention(q_nope, q_rope, k_nope, kr_all, v_a)
    else:
        o_a = sample_attention(q_nope, q_rope, k_nope, kr_all, v_a)
    v_n = rmsnorm(v, gm_v_norm)
    o_b = spatial_gating(u, v_n, gm_w_s, gm_b_s)
    mixed = jax.nn.sigmoid(g_a) * o_a + jax.nn.sigmoid(g_b) * o_b
    h = h + mixed @ w_o
    h = h + 0.5 * swiglu(rmsnorm(h, f2_norm), f2_wg, f2_wu, f2_wd)
    h = h + jax.nn.sigmoid(rmsnorm(h, ple_norm) @ ple_w_gate) * (p_l @ ple_w_proj)
    return h, c, kr, v_n


def setup_inputs(seed: int = 0) -> dict:
    key = jax.random.key(seed)
    ks = iter(jax.random.split(key, 40))

    def nrm(shape, scale=1.0):
        return jax.random.normal(next(ks), shape, jnp.float32) * scale

    def gain(n):
        return 1.0 + 0.05 * nrm((DEPTH, n))

    def lin(fan_in, *shape):
        return nrm((DEPTH, fan_in) + tuple(shape), fan_in ** -0.5)

    return {
        'x_prompt': nrm((BATCH, SEQ, D_MODEL)),
        'x_sample': nrm((DEC_BATCH, DEC_SEQ, D_MODEL)),
        'cache_kv_latent': nrm((DEPTH, DEC_BATCH, PAST_LEN, KV_LORA)),
        'cache_k_rope': nrm((DEPTH, DEC_BATCH, PAST_LEN, QK_ROPE)),
        'p_prompt': nrm((DEPTH, BATCH, SEQ, PLE_DIM)),
        'p_sample': nrm((DEPTH, DEC_BATCH, DEC_SEQ, PLE_DIM)),
        'ffn1_norm': gain(D_MODEL),
        'ffn1_w_gate': lin(D_MODEL, D_FF),
        'ffn1_w_up': lin(D_MODEL, D_FF),
        'ffn1_w_down': lin(D_FF, D_MODEL),
        'mix_norm': gain(D_MODEL),
        'w_in': lin(D_MODEL, D_IN),
        'q_a_norm': gain(Q_LORA),
        'w_uq': lin(Q_LORA, N_HEADS * (QK_NOPE + QK_ROPE)),
        'q_nope_norm': gain(QK_NOPE),
        'q_rope_norm': gain(QK_ROPE),
        'kv_a_norm': gain(KV_LORA),
        'k_rope_norm': gain(QK_ROPE),
        'w_uk': lin(KV_LORA, N_HEADS, QK_NOPE),
        'w_uv': lin(KV_LORA, N_HEADS, V_DIM),
        'k_nope_norm': gain(QK_NOPE),
        'gm_v_norm': gain(GM_WIDTH),
        'gm_w_s': nrm((DEPTH, GM_GROUPS, GM_CHUNK, GM_CHUNK), GM_CHUNK ** -0.5),
        'gm_b_s': 1.0 + 0.1 * nrm((DEPTH, GM_GROUPS, GM_CHUNK)),
        'w_o': lin(D_MODEL, D_MODEL),
        'ffn2_norm': gain(D_MODEL),
        'ffn2_w_gate': lin(D_MODEL, D_FF),
        'ffn2_w_up': lin(D_MODEL, D_FF),
        'ffn2_w_down': lin(D_FF, D_MODEL),
        'ple_norm': gain(D_MODEL),
        'ple_w_gate': lin(D_MODEL, D_MODEL),
        'ple_w_proj': lin(PLE_DIM, D_MODEL),
    }


def reference(x_prompt, x_sample, cache_kv_latent, cache_k_rope, p_prompt, p_sample,
              ffn1_norm, ffn1_w_gate, ffn1_w_up, ffn1_w_down, mix_norm, w_in,
              q_a_norm, w_uq, q_nope_norm, q_rope_norm, kv_a_norm, k_rope_norm,
              w_uk, w_uv, k_nope_norm, gm_v_norm, gm_w_s, gm_b_s, w_o,
              ffn2_norm, ffn2_w_gate, ffn2_w_up, ffn2_w_down,
              ple_norm, ple_w_gate, ple_w_proj):
    stacked = (ffn1_norm, ffn1_w_gate, ffn1_w_up, ffn1_w_down, mix_norm, w_in,
               q_a_norm, w_uq, q_nope_norm, q_rope_norm, kv_a_norm, k_rope_norm,
               w_uk, w_uv, k_nope_norm, gm_v_norm, gm_w_s, gm_b_s, w_o,
               ffn2_norm, ffn2_w_gate, ffn2_w_up, ffn2_w_down,
               ple_norm, ple_w_gate, ple_w_proj)
    seq_p = x_prompt.shape[1]
    past = cache_kv_latent.shape[2]
    dec = x_sample.shape[1]
    pos_p = jnp.arange(seq_p, dtype=jnp.int32)
    pos_s = past + jnp.arange(dec, dtype=jnp.int32)
    hp, hs = x_prompt, x_sample
    pc, pkr, sc, skr, sv = [], [], [], [], []
    for l in range(DEPTH):
        wl = tuple(w[l] for w in stacked)
        hp, c_p, kr_p, _ = run_layer(hp, p_prompt[l], pos_p, None, None, wl)
        hs, c_s, kr_s, v_s = run_layer(hs, p_sample[l], pos_s,
                                       cache_kv_latent[l], cache_k_rope[l], wl)
        pc.append(c_p)
        pkr.append(kr_p)
        sc.append(c_s)
        skr.append(kr_s)
        sv.append(v_s)
    prompt_kv_latent = jnp.stack(pc)
    prompt_k_rope = jnp.stack(pkr)
    sample_kv_latent = jnp.stack(sc)
    sample_k_rope = jnp.stack(skr)
    sample_gm_v = jnp.stack(sv)
    return (hp, hs, prompt_kv_latent, prompt_k_rope, sample_kv_latent, sample_k_rope, sample_gm_v)
```

```python
import functools

import jax
import jax.numpy as jnp
from jax import lax
from jax.experimental import pallas as pl
from jax.experimental.pallas import tpu as pltpu

D_MODEL = 1024
DEPTH = 2
CHUNK = 64
N_HEADS = 8
QK_NOPE = 128
QK_ROPE = 64
V_DIM = 128
Q_LORA = 384
KV_LORA = 512
GM_CHUNK = 128
GM_WIDTH = 1024
GM_GROUPS = 8
D_FF = 2816
PLE_DIM = 256
ROPE_THETA = 10000.0
EPS = 1e-6
SCALE = (QK_NOPE + QK_ROPE) ** -0.5

LANE = 128
HEAD_PAD = 2 * LANE
VMEM_LIMIT = 56 * 1024 * 1024
FF_CHUNK = 256
NEG = -1e30

F32 = jnp.float32
BF16 = jnp.bfloat16


def _dot(a, b):
    return jnp.dot(a, b, preferred_element_type=F32)


def _dot_nt(a, b):
    return lax.dot_general(a, b, (((1,), (1,)), ((), ())), preferred_element_type=F32)


def _rms(x, g, n=None):
    n = x.shape[-1] if n is None else n
    ms = jnp.sum(x * x, axis=-1, keepdims=True) * (1.0 / n)
    return x * lax.rsqrt(ms + EPS) * g


def _const_spec(shape):
    nd = len(shape)
    return pl.BlockSpec(shape, lambda *_: (0,) * nd, pipeline_mode=pl.Buffered(1))


def _params(*sem):
    return pltpu.CompilerParams(dimension_semantics=sem, vmem_limit_bytes=VMEM_LIMIT)


def _swiglu_half(x, g, wg_ref, wu_ref, wd_ref):
    n = _rms(x, g).astype(BF16)
    acc = jnp.zeros(x.shape, F32)
    for c in range(D_FF // FF_CHUNK):
        sl = slice(c * FF_CHUNK, (c + 1) * FF_CHUNK)
        gt = _dot(n, wg_ref[:, sl])
        up = _dot(n, wu_ref[:, sl])
        a = (gt * jax.nn.sigmoid(gt) * up).astype(BF16)
        acc = acc + _dot(a, wd_ref[sl, :])
    return x + 0.5 * acc


def _ffn_kernel(x_ref, g_ref, wg_ref, wu_ref, wd_ref, o_ref):
    o_ref[...] = _swiglu_half(x_ref[...], g_ref[...], wg_ref, wu_ref, wd_ref)


def _ffn(x, g, wg, wu, wd, tm):
    rows = x.shape[0]
    row = pl.BlockSpec((tm, D_MODEL), lambda i: (i, 0))
    return pl.pallas_call(
        _ffn_kernel,
        grid=(rows // tm,),
        in_specs=[row, _const_spec(g.shape), _const_spec(wg.shape),
                  _const_spec(wu.shape), _const_spec(wd.shape)],
        out_specs=row,
        out_shape=jax.ShapeDtypeStruct(x.shape, F32),
        compiler_params=_params("parallel"),
        name="ffn",
    )(x, g, wg, wu, wd)


def _rope(x, tab):
    return (x * tab[:, :LANE]
            + pltpu.roll(x, LANE - QK_ROPE // 2, 1) * tab[:, LANE:2 * LANE]
            + pltpu.roll(x, QK_ROPE // 2, 1) * tab[:, 2 * LANE:])


def _mix_kernel(h_ref, tab_ref, g_mix, w_q, w_kv, w_kr, w_u, w_v, w_ga, w_gb,
                g_qa, w_uq, g_qn, g_qr, g_kva, g_kr, w_uk, w_uv, g_kn, g_v,
                ws_ref, bs_ref, *out_refs, tm, emit_v, emit_vn):
    out_refs = list(out_refs)
    q_ref, k_ref, c_ref, kr_ref, sa_ref, gb_ref = out_refs[:6]
    rest = out_refs[6:]
    v_ref = rest.pop(0) if emit_v else None
    vn_ref = rest.pop(0) if emit_vn else None

    n = _rms(h_ref[...], g_mix[...]).astype(BF16)
    tab = tab_ref[...]

    kr = _rope(_rms(_dot(n, w_kr[...]), g_kr[...], QK_ROPE), tab)
    kr_ref[...] = kr[:, :QK_ROPE]

    c = _rms(_dot(n, w_kv[...]), g_kva[...])
    c_ref[...] = c
    cb = c.astype(BF16)
    kn = _dot(cb, w_uk[...])
    for h in range(N_HEADS):
        kh = _rms(kn[:, h * QK_NOPE:(h + 1) * QK_NOPE], g_kn[...])
        k_ref[h] = jnp.concatenate([kh, kr], axis=1).astype(BF16)
    if emit_v:
        va = _dot(cb, w_uv[...])
        for h in range(N_HEADS):
            v_ref[h] = va[:, h * V_DIM:(h + 1) * V_DIM].astype(BF16)

    ql = _rms(_dot(n, w_q[...]), g_qa[...]).astype(BF16)
    qf = _dot(ql, w_uq[...])
    for h in range(N_HEADS):
        blk = qf[:, h * HEAD_PAD:(h + 1) * HEAD_PAD]
        qn = _rms(blk[:, :QK_NOPE], g_qn[...])
        qr = _rope(_rms(blk[:, QK_NOPE:], g_qr[...], QK_ROPE), tab)
        q_ref[h] = (jnp.concatenate([qn, qr], axis=1) * SCALE).astype(BF16)

    vn = _rms(_dot(n, w_v[...]), g_v[...])
    if emit_vn:
        vn_ref[...] = vn
    vb = vn.astype(BF16)
    u = _dot(n, w_u[...])
    gbz = jax.nn.sigmoid(_dot(n, w_gb[...]))
    nc = tm // GM_CHUNK
    tril = (lax.broadcasted_iota(jnp.int32, (GM_CHUNK, GM_CHUNK), 0)
            >= lax.broadcasted_iota(jnp.int32, (GM_CHUNK, GM_CHUNK), 1))
    for g in range(GM_GROUPS):
        cs = slice(g * LANE, (g + 1) * LANE)
        w = jnp.where(tril, ws_ref[g], 0.0).astype(BF16)
        rhs = jnp.concatenate(
            [vb[c_ * GM_CHUNK:(c_ + 1) * GM_CHUNK, cs] for c_ in range(nc)], axis=1)
        mixed = _dot(w, rhs)
        for c_ in range(nc):
            rs = slice(c_ * GM_CHUNK, (c_ + 1) * GM_CHUNK)
            ob = u[rs, cs] * (mixed[:, c_ * LANE:(c_ + 1) * LANE] + bs_ref[g])
            gb_ref[rs, cs] = gbz[rs, cs] * ob
    sa_ref[...] = jax.nn.sigmoid(_dot(n, w_ga[...]))


def _mix_in(h, tab, wl, ws_eff, bs_eff, tm, emit_v, emit_vn):
    rows = h.shape[0]
    row = lambda w: pl.BlockSpec((tm, w), lambda i: (i, 0))
    head = lambda w: pl.BlockSpec((N_HEADS, tm, w), lambda i: (0, i, 0))
    consts = [wl["g_mix"], wl["w_q"], wl["w_kv"], wl["w_kr"], wl["w_u"], wl["w_v"],
              wl["w_ga"], wl["w_gb"], wl["g_qa"], wl["w_uq"], wl["g_qn"], wl["g_qr"],
              wl["g_kva"], wl["g_kr"], wl["w_uk"], wl["w_uv"], wl["g_kn"], wl["g_v"],
              ws_eff, bs_eff]
    out_shape = [jax.ShapeDtypeStruct((N_HEADS, rows, HEAD_PAD), BF16),
                 jax.ShapeDtypeStruct((N_HEADS, rows, HEAD_PAD), BF16),
                 jax.ShapeDtypeStruct((rows, KV_LORA), F32),
                 jax.ShapeDtypeStruct((rows, QK_ROPE), F32),
                 jax.ShapeDtypeStruct((rows, D_MODEL), F32),
                 jax.ShapeDtypeStruct((rows, D_MODEL), F32)]
    out_specs = [head(HEAD_PAD), head(HEAD_PAD), row(KV_LORA), row(QK_ROPE),
                 row(D_MODEL), row(D_MODEL)]
    if emit_v:
        out_shape.append(jax.ShapeDtypeStruct((N_HEADS, rows, V_DIM), BF16))
        out_specs.append(head(V_DIM))
    if emit_vn:
        out_shape.append(jax.ShapeDtypeStruct((rows, GM_WIDTH), F32))
        out_specs.append(row(GM_WIDTH))
    return pl.pallas_call(
        functools.partial(_mix_kernel, tm=tm, emit_v=emit_v, emit_vn=emit_vn),
        grid=(rows // tm,),
        in_specs=[row(D_MODEL), row(3 * LANE)] + [_const_spec(a.shape) for a in consts],
        out_specs=out_specs,
        out_shape=out_shape,
        compiler_params=_params("parallel"),
        name="mix_in",
    )(h, tab, *consts)


def _attn_kernel(q_ref, k_ref, v_ref, o_ref, *, tq, tk):
    i = pl.program_id(1)
    q = q_ref[0]
    ratio = tq // tk

    def step(j, carry, mask):
        m, l, acc = carry
        start = pl.multiple_of(j * tk, tk)
        s = _dot_nt(q, k_ref[0, pl.ds(start, tk), :])
        if mask is not None:
            s = jnp.where(mask, s, NEG)
        m_new = jnp.maximum(m, jnp.max(s, axis=-1, keepdims=True))
        alpha = jnp.exp(m - m_new)
        p = jnp.exp(s - m_new)
        l = alpha * l + jnp.sum(p, axis=-1, keepdims=True)
        acc = alpha * acc + _dot(p.astype(BF16), v_ref[0, pl.ds(start, tk), :])
        return m_new, l, acc

    carry = (jnp.full((tq, 1), NEG, F32), jnp.zeros((tq, 1), F32),
             jnp.zeros((tq, V_DIM), F32))
    carry = lax.fori_loop(0, i * ratio, lambda j, c: step(j, c, None), carry)
    rq = lax.broadcasted_iota(jnp.int32, (tq, tk), 0) // CHUNK
    ck = lax.broadcasted_iota(jnp.int32, (tq, tk), 1) // CHUNK
    for jj in range(ratio):
        carry = step(i * ratio + jj, carry, ck + jj * (tk // CHUNK) <= rq)
    m, l, acc = carry
    o_ref[...] = acc / l


def _prompt_attention(q, k, v, tq, tk):
    _, s, _ = q.shape
    return pl.pallas_call(
        functools.partial(_attn_kernel, tq=tq, tk=tk),
        grid=(N_HEADS, s // tq),
        in_specs=[pl.BlockSpec((1, tq, HEAD_PAD), lambda h, i: (h, i, 0)),
                  pl.BlockSpec((1, s, HEAD_PAD), lambda h, i: (h, 0, 0)),
                  pl.BlockSpec((1, s, V_DIM), lambda h, i: (h, 0, 0))],
        out_specs=pl.BlockSpec((tq, V_DIM), lambda h, i: (i, h)),
        out_shape=jax.ShapeDtypeStruct((s, N_HEADS * V_DIM), F32),
        compiler_params=_params("parallel", "arbitrary"),
        name="prompt_attn",
    )(q, k, v)


def _sample_attn_kernel(q_ref, kn_ref, cn_ref, cc_ref, ckr_ref, w_uk, w_uv, g_kn,
                        o_ref, s_scr, cb_scr, *, dec, past, tkc):
    nch = past // tkc
    qn = [q_ref[h][:, :QK_NOPE] for h in range(N_HEADS)]
    qr_all = jnp.concatenate([q_ref[h][:, QK_NOPE:] for h in range(N_HEADS)], axis=0)

    for j in range(nch):
        cb = cc_ref[0, j * tkc:(j + 1) * tkc, :].astype(BF16)
        cb_scr[j] = cb
        kn = _dot(cb, w_uk[...])
        s_rope = _dot_nt(qr_all, ckr_ref[0, j * tkc:(j + 1) * tkc, :])
        for h in range(N_HEADS):
            kh = _rms(kn[:, h * QK_NOPE:(h + 1) * QK_NOPE], g_kn[...]).astype(BF16)
            s_scr[j, h * dec:(h + 1) * dec, :] = (
                _dot_nt(qn[h], kh) + s_rope[h * dec:(h + 1) * dec])

    s_new = jnp.concatenate([_dot_nt(q_ref[h], kn_ref[h]) for h in range(N_HEADS)], axis=0)
    m = jnp.max(s_new, axis=-1, keepdims=True)
    for j in range(nch):
        m = jnp.maximum(m, jnp.max(s_scr[j], axis=-1, keepdims=True))
    p_n = jnp.exp(s_new - m)
    l = jnp.sum(p_n, axis=-1, keepdims=True)
    pc = _dot(p_n.astype(BF16), cn_ref[...].astype(BF16))
    for j in range(nch):
        p_c = jnp.exp(s_scr[j] - m)
        l = l + jnp.sum(p_c, axis=-1, keepdims=True)
        pc = pc + _dot(p_c.astype(BF16), cb_scr[j])
    pcb = (pc / l).astype(BF16)
    for h in range(N_HEADS):
        o_ref[:, h * V_DIM:(h + 1) * V_DIM] = _dot(
            pcb[h * dec:(h + 1) * dec], w_uv[:, h * V_DIM:(h + 1) * V_DIM])


def _sample_attention(q, k_new, c_new, cache_c, cache_kr, w_uk, w_uv, g_kn, dec):
    nb, past, _ = cache_c.shape
    rows = nb * dec
    tkc = 512
    return pl.pallas_call(
        functools.partial(_sample_attn_kernel, dec=dec, past=past, tkc=tkc),
        grid=(nb,),
        in_specs=[pl.BlockSpec((N_HEADS, dec, HEAD_PAD), lambda b: (0, b, 0)),
                  pl.BlockSpec((N_HEADS, dec, HEAD_PAD), lambda b: (0, b, 0)),
                  pl.BlockSpec((dec, KV_LORA), lambda b: (b, 0)),
                  pl.BlockSpec((1, past, KV_LORA), lambda b: (b, 0, 0)),
                  pl.BlockSpec((1, past, LANE), lambda b: (b, 0, 0)),
                  _const_spec(w_uk.shape), _const_spec(w_uv.shape),
                  _const_spec(g_kn.shape)],
        out_specs=pl.BlockSpec((dec, N_HEADS * V_DIM), lambda b: (b, 0)),
        out_shape=jax.ShapeDtypeStruct((rows, N_HEADS * V_DIM), F32),
        scratch_shapes=[pltpu.VMEM((past // tkc, N_HEADS * dec, tkc), F32),
                        pltpu.VMEM((past // tkc, tkc, KV_LORA), BF16)],
        compiler_params=_params("parallel"),
        name="sample_attn",
    )(q, k_new, c_new, cache_c, cache_kr, w_uk, w_uv, g_kn)


def _post_kernel(h_ref, sa_ref, oa_ref, gb_ref, p_ref, w_o, g_f2, wg, wu, wd,
                 g_ple, w_pg, w_pe, o_ref):
    mixed = (sa_ref[...] * oa_ref[...] + gb_ref[...]).astype(BF16)
    h = h_ref[...] + _dot(mixed, w_o[...])
    h = _swiglu_half(h, g_f2[...], wg, wu, wd)
    gate = jax.nn.sigmoid(_dot(_rms(h, g_ple[...]).astype(BF16), w_pg[...]))
    o_ref[...] = h + gate * _dot(p_ref[...].astype(BF16), w_pe[...])


def _post(h, sa, oa, gb, p, wl, tm):
    rows = h.shape[0]
    row = lambda w: pl.BlockSpec((tm, w), lambda i: (i, 0))
    consts = [wl["w_o"], wl["g_f2"], wl["f2_wg"], wl["f2_wu"], wl["f2_wd"],
              wl["g_ple"], wl["w_pg"], wl["w_pe"]]
    return pl.pallas_call(
        _post_kernel,
        grid=(rows // tm,),
        in_specs=[row(D_MODEL)] * 4 + [row(PLE_DIM)] + [_const_spec(a.shape) for a in consts],
        out_specs=row(D_MODEL),
        out_shape=jax.ShapeDtypeStruct(h.shape, F32),
        compiler_params=_params("parallel"),
        name="post",
    )(h, sa, oa, gb, p, *consts)


def _rope_table(pos):
    half = QK_ROPE // 2
    inv = ROPE_THETA ** (-jnp.arange(half, dtype=F32) / half)
    ang = pos.astype(F32)[:, None] * inv[None, :]
    cos, sin = jnp.cos(ang), jnp.sin(ang)
    z = jnp.zeros_like(cos)
    return jnp.concatenate([cos, cos, z, z, -sin, z, z, z, z, sin, z, z], axis=1)


def _layer_weights(l, w):
    row = lambda a: a[l][None, :].astype(F32)
    pad_lane = lambda a: jnp.pad(a, ((0, 0), (0, LANE - a.shape[1])))
    b = lambda a: a.astype(BF16)
    w_in = w["w_in"][l]
    o_kv = Q_LORA
    o_kr = o_kv + KV_LORA
    o_u = o_kr + QK_ROPE
    o_v = o_u + GM_WIDTH
    o_ga = o_v + GM_WIDTH
    o_gb = o_ga + D_MODEL
    w_uq = w["w_uq"][l].reshape(Q_LORA, N_HEADS, QK_NOPE + QK_ROPE)
    w_uq = jnp.pad(w_uq, ((0, 0), (0, 0), (0, HEAD_PAD - QK_NOPE - QK_ROPE)))
    return dict(
        g_f1=row(w["ffn1_norm"]), f1_wg=b(w["ffn1_w_gate"][l]), f1_wu=b(w["ffn1_w_up"][l]),
        f1_wd=b(w["ffn1_w_down"][l]),
        g_mix=row(w["mix_norm"]),
        w_q=b(w_in[:, :o_kv]), w_kv=b(w_in[:, o_kv:o_kr]), w_kr=b(pad_lane(w_in[:, o_kr:o_u])),
        w_u=b(w_in[:, o_u:o_v]), w_v=b(w_in[:, o_v:o_ga]), w_ga=b(w_in[:, o_ga:o_gb]),
        w_gb=b(w_in[:, o_gb:]),
        g_qa=row(w["q_a_norm"]), w_uq=b(w_uq.reshape(Q_LORA, N_HEADS * HEAD_PAD)),
        g_qn=row(w["q_nope_norm"]), g_qr=pad_lane(row(w["q_rope_norm"])),
        g_kva=row(w["kv_a_norm"]), g_kr=pad_lane(row(w["k_rope_norm"])),
        w_uk=b(w["w_uk"][l].reshape(KV_LORA, N_HEADS * QK_NOPE)),
        w_uv=b(w["w_uv"][l].reshape(KV_LORA, N_HEADS * V_DIM)),
        g_kn=row(w["k_nope_norm"]), g_v=row(w["gm_v_norm"]),
        w_o=b(w["w_o"][l]),
        g_f2=row(w["ffn2_norm"]), f2_wg=b(w["ffn2_w_gate"][l]), f2_wu=b(w["ffn2_w_up"][l]),
        f2_wd=b(w["ffn2_w_down"][l]),
        g_ple=row(w["ple_norm"]), w_pg=b(w["ple_w_gate"][l]), w_pe=b(w["ple_w_proj"][l]),
    )


def _gate_operands(w_s, b_s, length):
    reps = GM_CHUNK // length
    eye = jnp.eye(reps, dtype=w_s.dtype)
    ws = jnp.einsum("ab,gts->gatbs", eye, w_s[:, :length, :length])
    ws = ws.reshape(GM_GROUPS, GM_CHUNK, GM_CHUNK)
    bs = jnp.tile(b_s[:, :length], (1, reps))
    return ws, jnp.broadcast_to(bs[:, :, None], (GM_GROUPS, GM_CHUNK, LANE))


def kernel(x_prompt, x_sample, cache_kv_latent, cache_k_rope, p_prompt, p_sample, ffn1_norm, ffn1_w_gate, ffn1_w_up, ffn1_w_down, mix_norm, w_in, q_a_norm, w_uq, q_nope_norm, q_rope_norm, kv_a_norm, k_rope_norm, w_uk, w_uv, k_nope_norm, gm_v_norm, gm_w_s, gm_b_s, w_o, ffn2_norm, ffn2_w_gate, ffn2_w_up, ffn2_w_down, ple_norm, ple_w_gate, ple_w_proj):
    w = dict(ffn1_norm=ffn1_norm, ffn1_w_gate=ffn1_w_gate, ffn1_w_up=ffn1_w_up,
             ffn1_w_down=ffn1_w_down, mix_norm=mix_norm, w_in=w_in, q_a_norm=q_a_norm,
             w_uq=w_uq, q_nope_norm=q_nope_norm, q_rope_norm=q_rope_norm,
             kv_a_norm=kv_a_norm, k_rope_norm=k_rope_norm, w_uk=w_uk, w_uv=w_uv,
             k_nope_norm=k_nope_norm, gm_v_norm=gm_v_norm, w_o=w_o, ffn2_norm=ffn2_norm,
             ffn2_w_gate=ffn2_w_gate, ffn2_w_up=ffn2_w_up, ffn2_w_down=ffn2_w_down,
             ple_norm=ple_norm, ple_w_gate=ple_w_gate, ple_w_proj=ple_w_proj)
    batch, seq, _ = x_prompt.shape
    nb, dec, _ = x_sample.shape
    past = cache_kv_latent.shape[2]
    assert batch == 1 and seq % 512 == 0 and (nb * dec) % 256 == 0 and GM_CHUNK % dec == 0

    tab_p = _rope_table(jnp.arange(seq, dtype=jnp.int32))
    tab_s = jnp.tile(_rope_table(past + jnp.arange(dec, dtype=jnp.int32)), (nb, 1))
    hp = x_prompt.reshape(seq, D_MODEL)
    hs = x_sample.reshape(nb * dec, D_MODEL)
    tm_p, tm_s = 256, 256

    outs = {k: [] for k in ("pc", "pkr", "sc", "skr", "sv")}
    for l in range(DEPTH):
        wl = _layer_weights(l, w)
        ws_p, bs_p = _gate_operands(gm_w_s[l], gm_b_s[l], GM_CHUNK)
        ws_s, bs_s = _gate_operands(gm_w_s[l], gm_b_s[l], dec)
        ckr = jnp.pad(cache_k_rope[l].astype(BF16), ((0, 0), (0, 0), (0, LANE - QK_ROPE)))

        hp = _ffn(hp, wl["g_f1"], wl["f1_wg"], wl["f1_wu"], wl["f1_wd"], 512)
        q, k, c, kr, sa, gb, v = _mix_in(hp, tab_p, wl, ws_p, bs_p, tm_p, True, False)
        oa = _prompt_attention(q, k, v, 512, 512)
        hp = _post(hp, sa, oa, gb, p_prompt[l].reshape(seq, PLE_DIM), wl, 256)
        outs["pc"].append(c.reshape(batch, seq, KV_LORA))
        outs["pkr"].append(kr.reshape(batch, seq, QK_ROPE))

        hs = _ffn(hs, wl["g_f1"], wl["f1_wg"], wl["f1_wu"], wl["f1_wd"], 256)
        q, k, c, kr, sa, gb, vn = _mix_in(hs, tab_s, wl, ws_s, bs_s, tm_s, False, True)
        oa = _sample_attention(q, k, c, cache_kv_latent[l], ckr, wl["w_uk"], wl["w_uv"],
                               wl["g_kn"], dec)
        hs = _post(hs, sa, oa, gb, p_sample[l].reshape(nb * dec, PLE_DIM), wl, 256)
        outs["sc"].append(c.reshape(nb, dec, KV_LORA))
        outs["skr"].append(kr.reshape(nb, dec, QK_ROPE))
        outs["sv"].append(vn.reshape(nb, dec, GM_WIDTH))

    return (hp.reshape(batch, seq, D_MODEL), hs.reshape(nb, dec, D_MODEL),
            jnp.stack(outs["pc"]), jnp.stack(outs["pkr"]), jnp.stack(outs["sc"]),
            jnp.stack(outs["skr"]), jnp.stack(outs["sv"]))
```

```python
import functools
import math

import jax
import jax.numpy as jnp
from jax import lax
from jax.experimental import pallas as pl
from jax.experimental.pallas import tpu as pltpu

D_MODEL = 1024
DEPTH = 2
CHUNK = 64
N_HEADS = 8
QK_NOPE = 128
QK_ROPE = 64
V_DIM = 128
Q_LORA = 384
KV_LORA = 512
GM_CHUNK = 128
GM_WIDTH = 1024
GM_GROUPS = 8
D_FF = 2816
PLE_DIM = 256
ROPE_THETA = 10000.0
EPS = 1e-6
SCALE = (QK_NOPE + QK_ROPE) ** -0.5
Q_SCALE = SCALE * math.log2(math.e)

LANE = 128
HEAD_PAD = 2 * LANE
HALF = QK_ROPE // 2
VMEM_LIMIT = 56 * 1024 * 1024
FF_CHUNK = 256
ATT_TQ = 512
ATT_SUB = 256
ATT_TK = 512
ATT_HEADS = 2
ATT_MAX_JUMP = 64.0
NEG = -1e30

F32 = jnp.float32
BF16 = jnp.bfloat16


def _dot(a, b):
    return jnp.dot(a, b, preferred_element_type=F32)


def _dot_nt(a, b):
    return lax.dot_general(a, b, (((1,), (1,)), ((), ())), preferred_element_type=F32)


def _rms(x, g, n=None):
    n = x.shape[-1] if n is None else n
    ms = jnp.sum(x * x, axis=-1, keepdims=True) * (1.0 / n)
    return x * lax.rsqrt(ms + EPS) * g


def _const_spec(shape):
    nd = len(shape)
    return pl.BlockSpec(shape, lambda *_: (0,) * nd, pipeline_mode=pl.Buffered(1))


def _params(*sem):
    return pltpu.CompilerParams(dimension_semantics=sem, vmem_limit_bytes=VMEM_LIMIT)


def _swiglu_half(x, g, wg_ref, wu_ref, wd_ref):
    n = _rms(x, g).astype(BF16)
    acc = jnp.zeros(x.shape, F32)
    for c in range(D_FF // FF_CHUNK):
        sl = slice(c * FF_CHUNK, (c + 1) * FF_CHUNK)
        gt = _dot(n, wg_ref[:, sl])
        up = _dot(n, wu_ref[:, sl])
        a = (gt * jax.nn.sigmoid(gt) * up).astype(BF16)
        acc = acc + _dot(a, wd_ref[sl, :])
    return x + 0.5 * acc


def _ffn_kernel(x_ref, g_ref, wg_ref, wu_ref, wd_ref, o_ref):
    o_ref[...] = _swiglu_half(x_ref[...], g_ref[...], wg_ref, wu_ref, wd_ref)


def _ffn(x, g, wg, wu, wd, tm):
    rows = x.shape[0]
    row = pl.BlockSpec((tm, D_MODEL), lambda i: (i, 0))
    return pl.pallas_call(
        _ffn_kernel,
        grid=(rows // tm,),
        in_specs=[row, _const_spec(g.shape), _const_spec(wg.shape),
                  _const_spec(wu.shape), _const_spec(wd.shape)],
        out_specs=row,
        out_shape=jax.ShapeDtypeStruct(x.shape, F32),
        compiler_params=_params("parallel"),
        name="ffn",
    )(x, g, wg, wu, wd)


def _rope(x, tab):
    return (x * tab[:, :LANE]
            + pltpu.roll(x, LANE - HALF, 1) * tab[:, LANE:2 * LANE]
            + pltpu.roll(x, HALF, 1) * tab[:, 2 * LANE:])


def _mix_kernel(h_ref, tab_ref, tabt_ref, g_mix, w_q, w_kv, w_kr, w_u, w_v, w_ga, w_gb,
                g_qa, w_uqt, g_qt, g_kva, g_kr, w_uk, w_uvt, g_kn, g_v,
                ws_ref, bs_ref, *out_refs, tm, emit_v, emit_vn):
    out_refs = list(out_refs)
    qt_ref, k_ref, c_ref, kr_ref, sa_ref, gb_ref = out_refs[:6]
    rest = out_refs[6:]
    vt_ref = rest.pop(0) if emit_v else None
    vn_ref = rest.pop(0) if emit_vn else None

    n = _rms(h_ref[...], g_mix[...]).astype(BF16)

    kr = _rope(_rms(_dot(n, w_kr[...]), g_kr[...], QK_ROPE), tab_ref[...])
    kr_ref[...] = kr[:, :QK_ROPE]

    c = _rms(_dot(n, w_kv[...]), g_kva[...])
    c_ref[...] = c
    cb = c.astype(BF16)
    kn = _dot(cb, w_uk[...])
    for h in range(N_HEADS):
        kh = _rms(kn[:, h * QK_NOPE:(h + 1) * QK_NOPE], g_kn[...])
        k_ref[h] = jnp.concatenate([kh, kr], axis=1).astype(BF16)
    if emit_v:
        vt = _dot_nt(w_uvt[...], cb)
        for h in range(N_HEADS):
            vt_ref[h, 0] = vt[h * V_DIM:(h + 1) * V_DIM].astype(BF16)

    ql = _rms(_dot(n, w_q[...]), g_qa[...]).astype(BF16)
    qft = _dot_nt(w_uqt[...], ql)
    cos, sin = tabt_ref[:HALF, :], tabt_ref[HALF:, :]
    g_nope = g_qt[:QK_NOPE, :]
    g_r1 = g_qt[QK_NOPE:QK_NOPE + HALF, :]
    g_r2 = g_qt[QK_NOPE + HALF:, :]
    zpad = jnp.zeros((HEAD_PAD - QK_NOPE - QK_ROPE, tm), F32)
    for h in range(N_HEADS):
        blk = qft[h * HEAD_PAD:(h + 1) * HEAD_PAD]
        nope = blk[:QK_NOPE]
        x1 = blk[QK_NOPE:QK_NOPE + HALF]
        x2 = blk[QK_NOPE + HALF:QK_NOPE + QK_ROPE]
        rn = lax.rsqrt(jnp.sum(nope * nope, axis=0, keepdims=True) * (1.0 / QK_NOPE) + EPS)
        rr = lax.rsqrt((jnp.sum(x1 * x1, axis=0, keepdims=True)
                        + jnp.sum(x2 * x2, axis=0, keepdims=True)) * (1.0 / QK_ROPE) + EPS)
        y1 = x1 * rr * g_r1
        y2 = x2 * rr * g_r2
        qh = jnp.concatenate([nope * rn * g_nope, y1 * cos - y2 * sin,
                              y1 * sin + y2 * cos, zpad], axis=0)
        qt_ref[h * HEAD_PAD:(h + 1) * HEAD_PAD, :] = (qh * Q_SCALE).astype(BF16)

    vn = _rms(_dot(n, w_v[...]), g_v[...])
    if emit_vn:
        vn_ref[...] = vn
    vb = vn.astype(BF16)
    u = _dot(n, w_u[...])
    gbz = jax.nn.sigmoid(_dot(n, w_gb[...]))
    nc = tm // GM_CHUNK
    tril = (lax.broadcasted_iota(jnp.int32, (GM_CHUNK, GM_CHUNK), 0)
            >= lax.broadcasted_iota(jnp.int32, (GM_CHUNK, GM_CHUNK), 1))
    for g in range(GM_GROUPS):
        cs = slice(g * LANE, (g + 1) * LANE)
        w = jnp.where(tril, ws_ref[g], 0.0).astype(BF16)
        rhs = jnp.concatenate(
            [vb[c_ * GM_CHUNK:(c_ + 1) * GM_CHUNK, cs] for c_ in range(nc)], axis=1)
        mixed = _dot(w, rhs)
        for c_ in range(nc):
            rs = slice(c_ * GM_CHUNK, (c_ + 1) * GM_CHUNK)
            ob = u[rs, cs] * (mixed[:, c_ * LANE:(c_ + 1) * LANE] + bs_ref[g])
            gb_ref[rs, cs] = gbz[rs, cs] * ob
    sa_ref[...] = jax.nn.sigmoid(_dot(n, w_ga[...]))


def _mix_in(h, tab, tabt, wl, ws_eff, bs_eff, tm, emit_v, emit_vn):
    rows = h.shape[0]
    row = lambda w: pl.BlockSpec((tm, w), lambda i: (i, 0))
    col = lambda r: pl.BlockSpec((r, tm), lambda i: (0, i))
    head = lambda w: pl.BlockSpec((N_HEADS, tm, w), lambda i: (0, i, 0))
    g_qt = jnp.broadcast_to(wl["g_qt"], (QK_NOPE + QK_ROPE, tm))
    consts = [wl["g_mix"], wl["w_q"], wl["w_kv"], wl["w_kr"], wl["w_u"], wl["w_v"],
              wl["w_ga"], wl["w_gb"], wl["g_qa"], wl["w_uqt"], g_qt,
              wl["g_kva"], wl["g_kr"], wl["w_uk"], wl["w_uvt"], wl["g_kn"], wl["g_v"],
              ws_eff, bs_eff]
    out_shape = [jax.ShapeDtypeStruct((N_HEADS * HEAD_PAD, rows), BF16),
                 jax.ShapeDtypeStruct((N_HEADS, rows, HEAD_PAD), BF16),
                 jax.ShapeDtypeStruct((rows, KV_LORA), F32),
                 jax.ShapeDtypeStruct((rows, QK_ROPE), F32),
                 jax.ShapeDtypeStruct((rows, D_MODEL), F32),
                 jax.ShapeDtypeStruct((rows, D_MODEL), F32)]
    out_specs = [col(N_HEADS * HEAD_PAD), head(HEAD_PAD), row(KV_LORA), row(QK_ROPE),
                 row(D_MODEL), row(D_MODEL)]
    if emit_v:
        per = ATT_TK // tm
        out_shape.append(jax.ShapeDtypeStruct((N_HEADS, rows // ATT_TK, V_DIM, ATT_TK), BF16))
        out_specs.append(pl.BlockSpec((N_HEADS, 1, V_DIM, tm),
                                      lambda i: (0, i // per, 0, i % per)))
    if emit_vn:
        out_shape.append(jax.ShapeDtypeStruct((rows, GM_WIDTH), F32))
        out_specs.append(row(GM_WIDTH))
    return pl.pallas_call(
        functools.partial(_mix_kernel, tm=tm, emit_v=emit_v, emit_vn=emit_vn),
        grid=(rows // tm,),
        in_specs=[row(D_MODEL), row(3 * LANE), col(QK_ROPE)]
        + [_const_spec(a.shape) for a in consts],
        out_specs=out_specs,
        out_shape=out_shape,
        compiler_params=_params("parallel"),
        name="mix_in",
    )(h, tab, tabt, *consts)


def _attn_kernel(qt_ref, k_ref, vt_ref, o_ref):
    tq, tk, nh = ATT_TQ, ATT_TK, ATT_HEADS
    i = pl.program_id(1)
    ns, ws = tq // ATT_SUB, ATT_SUB
    chains = [(h, a) for h in range(nh) for a in range(ns)]
    qts = {(h, a): qt_ref[h * HEAD_PAD:(h + 1) * HEAD_PAD, a * ws:(a + 1) * ws]
           for h, a in chains}

    def scores(j, ch):
        start = pl.multiple_of(j * tk, tk)
        return _dot(k_ref[ch[0], pl.ds(start, tk), :], qts[ch])

    dk = (lax.broadcasted_iota(jnp.int32, (tk, ws), 0) // CHUNK
          - lax.broadcasted_iota(jnp.int32, (tk, ws), 1) // CHUNK)
    masks = [dk <= a * (ws // CHUNK) for a in range(ns)]

    def write(ch, l, acc):
        h, a = ch
        o_ref[a * ws:(a + 1) * ws, h * V_DIM:(h + 1) * V_DIM] = (acc / l).T

    def update(st, state):
        m, l, acc, jump = state
        p = jnp.exp2(st - m)
        m_new = jnp.maximum(m, jnp.max(st, axis=0, keepdims=True))
        l = l + jnp.sum(p, axis=0, keepdims=True)
        return m_new, l, p.astype(BF16), jnp.maximum(jump, m_new - m)

    def body(j, states):
        sts = [scores(j, ch) for ch in chains]
        ups = [update(st, state) for st, state in zip(sts, states)]
        new_states = []
        for ch, state, (m, l, p, jump) in zip(chains, states, ups):
            acc = state[2] + _dot(vt_ref[ch[0], j], p)
            alpha = jnp.exp2(state[0] - m)
            new_states.append((m, alpha * l, alpha * acc, jump))
        return tuple(new_states)

    init = tuple(
        (jnp.max(_dot(k_ref[ch[0], 0:CHUNK, :], qts[ch]), axis=0, keepdims=True),
         jnp.zeros((1, ws), F32), jnp.zeros((V_DIM, ws), F32), jnp.zeros((1, ws), F32))
        for ch in chains)
    states = lax.fori_loop(0, i, body, init)
    worst = jnp.zeros((1, ws), F32)
    for ch, state in zip(chains, states):
        st = jnp.where(masks[ch[1]], scores(i, ch), NEG)
        _, l, p, jump = update(st, state)
        write(ch, l, state[2] + _dot(vt_ref[ch[0], i], p))
        worst = jnp.maximum(worst, jump)

    @pl.when(jnp.max(worst) > ATT_MAX_JUMP)
    def _():
        def exact(st, state):
            m, l, acc = state
            m_new = jnp.maximum(m, jnp.max(st, axis=0, keepdims=True))
            alpha = jnp.exp2(m - m_new)
            p = jnp.exp2(st - m_new)
            return m_new, alpha * l + jnp.sum(p, axis=0, keepdims=True), alpha * acc, p

        for ch in chains:
            def ebody(j, state, ch=ch):
                m, l, acc, p = exact(scores(j, ch), state)
                return m, l, acc + _dot(vt_ref[ch[0], j], p.astype(BF16))

            state = lax.fori_loop(0, i, ebody, (jnp.full((1, ws), NEG, F32),
                                               jnp.zeros((1, ws), F32),
                                               jnp.zeros((V_DIM, ws), F32)))
            st = jnp.where(masks[ch[1]], scores(i, ch), NEG)
            _, l, acc, p = exact(st, state)
            write(ch, l, acc + _dot(vt_ref[ch[0], i], p.astype(BF16)))


def _prompt_attention(qt, k, vt):
    s = k.shape[1]
    nh = ATT_HEADS
    once = pl.Buffered(1)
    return pl.pallas_call(
        _attn_kernel,
        grid=(N_HEADS // nh, s // ATT_TQ),
        in_specs=[pl.BlockSpec((nh * HEAD_PAD, ATT_TQ), lambda h, i: (h, i)),
                  pl.BlockSpec((nh, s, HEAD_PAD), lambda h, i: (h, 0, 0), pipeline_mode=once),
                  pl.BlockSpec((nh, s // ATT_TK, V_DIM, ATT_TK), lambda h, i: (h, 0, 0, 0),
                               pipeline_mode=once)],
        out_specs=pl.BlockSpec((ATT_TQ, nh * V_DIM), lambda h, i: (i, h)),
        out_shape=jax.ShapeDtypeStruct((s, N_HEADS * V_DIM), F32),
        compiler_params=_params("parallel", "arbitrary"),
        name="prompt_attn",
    )(qt, k, vt)


def _sample_attn_kernel(q_ref, kn_ref, cn_ref, cc_ref, ckr_ref, w_uk, w_uv, g_kn,
                        o_ref, s_scr, cb_scr, *, dec, past, tkc):
    nch = past // tkc
    qn = [q_ref[h][:, :QK_NOPE] for h in range(N_HEADS)]
    qr_all = jnp.concatenate(
        [q_ref[h][:, QK_NOPE:QK_NOPE + QK_ROPE] for h in range(N_HEADS)], axis=0)

    for j in range(nch):
        cb = cc_ref[0, 0, j * tkc:(j + 1) * tkc, :].astype(BF16)
        cb_scr[j] = cb
        kn = _dot(cb, w_uk[...])
        s_rope = _dot_nt(qr_all, ckr_ref[0, 0, j * tkc:(j + 1) * tkc, :].astype(BF16))
        for h in range(N_HEADS):
            kh = _rms(kn[:, h * QK_NOPE:(h + 1) * QK_NOPE], g_kn[...]).astype(BF16)
            s_scr[j, h * dec:(h + 1) * dec, :] = (
                _dot_nt(qn[h], kh) + s_rope[h * dec:(h + 1) * dec])

    s_new = jnp.concatenate([_dot_nt(q_ref[h], kn_ref[h]) for h in range(N_HEADS)], axis=0)
    m = jnp.max(s_new, axis=-1, keepdims=True)
    for j in range(nch):
        m = jnp.maximum(m, jnp.max(s_scr[j], axis=-1, keepdims=True))
    p_n = jnp.exp2(s_new - m)
    l = jnp.sum(p_n, axis=-1, keepdims=True)
    pc = _dot(p_n.astype(BF16), cn_ref[...].astype(BF16))
    for j in range(nch):
        p_c = jnp.exp2(s_scr[j] - m)
        l = l + jnp.sum(p_c, axis=-1, keepdims=True)
        pc = pc + _dot(p_c.astype(BF16), cb_scr[j])
    pcb = (pc / l).astype(BF16)
    for h in range(N_HEADS):
        o_ref[:, h * V_DIM:(h + 1) * V_DIM] = _dot(
            pcb[h * dec:(h + 1) * dec], w_uv[:, h * V_DIM:(h + 1) * V_DIM])


def _sample_attention(layer, q, k_new, c_new, cache_c, cache_kr, w_uk, w_uv, g_kn, dec):
    _, nb, past, _ = cache_c.shape
    rows = nb * dec
    tkc = 512
    return pl.pallas_call(
        functools.partial(_sample_attn_kernel, dec=dec, past=past, tkc=tkc),
        grid=(nb,),
        in_specs=[pl.BlockSpec((N_HEADS, dec, HEAD_PAD), lambda b: (0, b, 0)),
                  pl.BlockSpec((N_HEADS, dec, HEAD_PAD), lambda b: (0, b, 0)),
                  pl.BlockSpec((dec, KV_LORA), lambda b: (b, 0)),
                  pl.BlockSpec((1, 1, past, KV_LORA), lambda b: (layer, b, 0, 0)),
                  pl.BlockSpec((1, 1, past, QK_ROPE), lambda b: (layer, b, 0, 0)),
                  _const_spec(w_uk.shape), _const_spec(w_uv.shape),
                  _const_spec(g_kn.shape)],
        out_specs=pl.BlockSpec((dec, N_HEADS * V_DIM), lambda b: (b, 0)),
        out_shape=jax.ShapeDtypeStruct((rows, N_HEADS * V_DIM), F32),
        scratch_shapes=[pltpu.VMEM((past // tkc, N_HEADS * dec, tkc), F32),
                        pltpu.VMEM((past // tkc, tkc, KV_LORA), BF16)],
        compiler_params=_params("parallel"),
        name="sample_attn",
    )(q, k_new, c_new, cache_c, cache_kr, w_uk, w_uv, g_kn)


def _post_kernel(h_ref, sa_ref, oa_ref, gb_ref, p_ref, w_o, g_f2, wg, wu, wd,
                 g_ple, w_pg, w_pe, o_ref):
    mixed = (sa_ref[...] * oa_ref[...] + gb_ref[...]).astype(BF16)
    h = h_ref[...] + _dot(mixed, w_o[...])
    h = _swiglu_half(h, g_f2[...], wg, wu, wd)
    gate = jax.nn.sigmoid(_dot(_rms(h, g_ple[...]).astype(BF16), w_pg[...]))
    o_ref[...] = h + gate * _dot(p_ref[0].astype(BF16), w_pe[...])


def _post(layer, h, sa, oa, gb, p, wl, tm):
    rows = h.shape[0]
    row = lambda w: pl.BlockSpec((tm, w), lambda i: (i, 0))
    consts = [wl["w_o"], wl["g_f2"], wl["f2_wg"], wl["f2_wu"], wl["f2_wd"],
              wl["g_ple"], wl["w_pg"], wl["w_pe"]]
    return pl.pallas_call(
        _post_kernel,
        grid=(rows // tm,),
        in_specs=[row(D_MODEL)] * 4
        + [pl.BlockSpec((1, tm, PLE_DIM), lambda i: (layer, i, 0))]
        + [_const_spec(a.shape) for a in consts],
        out_specs=row(D_MODEL),
        out_shape=jax.ShapeDtypeStruct(h.shape, F32),
        compiler_params=_params("parallel"),
        name="post",
    )(h, sa, oa, gb, p, *consts)


def _rope_tables(pos):
    inv = ROPE_THETA ** (-jnp.arange(HALF, dtype=F32) / HALF)
    ang = pos.astype(F32)[:, None] * inv[None, :]
    cos, sin = jnp.cos(ang), jnp.sin(ang)
    z = jnp.zeros_like(cos)
    tab = jnp.concatenate([cos, cos, z, z, -sin, z, z, z, z, sin, z, z], axis=1)
    tabt = jnp.concatenate([cos, sin], axis=1).T
    return tab, tabt


def _layer_weights(l, w):
    row = lambda a: a[l][None, :].astype(F32)
    pad_lane = lambda a: jnp.pad(a, ((0, 0), (0, LANE - a.shape[1])))
    b = lambda a: a.astype(BF16)
    w_in = w["w_in"][l]
    o_kv = Q_LORA
    o_kr = o_kv + KV_LORA
    o_u = o_kr + QK_ROPE
    o_v = o_u + GM_WIDTH
    o_ga = o_v + GM_WIDTH
    o_gb = o_ga + D_MODEL
    w_uq = w["w_uq"][l].reshape(Q_LORA, N_HEADS, QK_NOPE + QK_ROPE)
    w_uq = jnp.pad(w_uq, ((0, 0), (0, 0), (0, HEAD_PAD - QK_NOPE - QK_ROPE)))
    g_qt = jnp.concatenate([w["q_nope_norm"][l], w["q_rope_norm"][l]])[:, None].astype(F32)
    return dict(
        g_f1=row(w["ffn1_norm"]), f1_wg=b(w["ffn1_w_gate"][l]), f1_wu=b(w["ffn1_w_up"][l]),
        f1_wd=b(w["ffn1_w_down"][l]),
        g_mix=row(w["mix_norm"]),
        w_q=b(w_in[:, :o_kv]), w_kv=b(w_in[:, o_kv:o_kr]), w_kr=b(pad_lane(w_in[:, o_kr:o_u])),
        w_u=b(w_in[:, o_u:o_v]), w_v=b(w_in[:, o_v:o_ga]), w_ga=b(w_in[:, o_ga:o_gb]),
        w_gb=b(w_in[:, o_gb:]),
        g_qa=row(w["q_a_norm"]),
        w_uqt=b(w_uq.reshape(Q_LORA, N_HEADS * HEAD_PAD).T), g_qt=g_qt,
        g_kva=row(w["kv_a_norm"]), g_kr=pad_lane(row(w["k_rope_norm"])),
        w_uk=b(w["w_uk"][l].reshape(KV_LORA, N_HEADS * QK_NOPE)),
        w_uv=b(w["w_uv"][l].reshape(KV_LORA, N_HEADS * V_DIM)),
        w_uvt=b(w["w_uv"][l].reshape(KV_LORA, N_HEADS * V_DIM).T),
        g_kn=row(w["k_nope_norm"]), g_v=row(w["gm_v_norm"]),
        w_o=b(w["w_o"][l]),
        g_f2=row(w["ffn2_norm"]), f2_wg=b(w["ffn2_w_gate"][l]), f2_wu=b(w["ffn2_w_up"][l]),
        f2_wd=b(w["ffn2_w_down"][l]),
        g_ple=row(w["ple_norm"]), w_pg=b(w["ple_w_gate"][l]), w_pe=b(w["ple_w_proj"][l]),
    )


def _gate_operands(w_s, b_s, length):
    reps = GM_CHUNK // length
    eye = jnp.eye(reps, dtype=w_s.dtype)
    ws = jnp.einsum("ab,gts->gatbs", eye, w_s[:, :length, :length])
    ws = ws.reshape(GM_GROUPS, GM_CHUNK, GM_CHUNK)
    bs = jnp.tile(b_s[:, :length], (1, reps))
    return ws, jnp.broadcast_to(bs[:, :, None], (GM_GROUPS, GM_CHUNK, LANE))


def kernel(x_prompt, x_sample, cache_kv_latent, cache_k_rope, p_prompt, p_sample, ffn1_norm, ffn1_w_gate, ffn1_w_up, ffn1_w_down, mix_norm, w_in, q_a_norm, w_uq, q_nope_norm, q_rope_norm, kv_a_norm, k_rope_norm, w_uk, w_uv, k_nope_norm, gm_v_norm, gm_w_s, gm_b_s, w_o, ffn2_norm, ffn2_w_gate, ffn2_w_up, ffn2_w_down, ple_norm, ple_w_gate, ple_w_proj):
    w = dict(ffn1_norm=ffn1_norm, ffn1_w_gate=ffn1_w_gate, ffn1_w_up=ffn1_w_up,
             ffn1_w_down=ffn1_w_down, mix_norm=mix_norm, w_in=w_in, q_a_norm=q_a_norm,
             w_uq=w_uq, q_nope_norm=q_nope_norm, q_rope_norm=q_rope_norm,
             kv_a_norm=kv_a_norm, k_rope_norm=k_rope_norm, w_uk=w_uk, w_uv=w_uv,
             k_nope_norm=k_nope_norm, gm_v_norm=gm_v_norm, w_o=w_o, ffn2_norm=ffn2_norm,
             ffn2_w_gate=ffn2_w_gate, ffn2_w_up=ffn2_w_up, ffn2_w_down=ffn2_w_down,
             ple_norm=ple_norm, ple_w_gate=ple_w_gate, ple_w_proj=ple_w_proj)
    batch, seq, _ = x_prompt.shape
    nb, dec, _ = x_sample.shape
    past = cache_kv_latent.shape[2]
    rows_s = nb * dec
    assert batch == 1 and seq % ATT_TK == 0 and rows_s % 256 == 0 and GM_CHUNK % dec == 0

    tab_p, tabt_p = _rope_tables(jnp.arange(seq, dtype=jnp.int32))
    tab_s, tabt_s = _rope_tables(past + jnp.arange(dec, dtype=jnp.int32))
    tab_s, tabt_s = jnp.tile(tab_s, (nb, 1)), jnp.tile(tabt_s, (1, nb))
    hp = x_prompt.reshape(seq, D_MODEL)
    hs = x_sample.reshape(rows_s, D_MODEL)
    pp = p_prompt.reshape(DEPTH, seq, PLE_DIM)
    ps = p_sample.reshape(DEPTH, rows_s, PLE_DIM)

    outs = {k: [] for k in ("pc", "pkr", "sc", "skr", "sv")}
    for l in range(DEPTH):
        wl = _layer_weights(l, w)
        ws_p, bs_p = _gate_operands(gm_w_s[l], gm_b_s[l], GM_CHUNK)
        ws_s, bs_s = _gate_operands(gm_w_s[l], gm_b_s[l], dec)

        hp = _ffn(hp, wl["g_f1"], wl["f1_wg"], wl["f1_wu"], wl["f1_wd"], 512)
        qt, k, c, kr, sa, gb, vt = _mix_in(hp, tab_p, tabt_p, wl, ws_p, bs_p, 256, True, False)
        oa = _prompt_attention(qt, k, vt)
        hp = _post(l, hp, sa, oa, gb, pp, wl, 256)
        outs["pc"].append(c.reshape(batch, seq, KV_LORA))
        outs["pkr"].append(kr.reshape(batch, seq, QK_ROPE))

        hs = _ffn(hs, wl["g_f1"], wl["f1_wg"], wl["f1_wu"], wl["f1_wd"], 256)
        qt, k, c, kr, sa, gb, vn = _mix_in(hs, tab_s, tabt_s, wl, ws_s, bs_s, 256, False, True)
        q = qt.reshape(N_HEADS, HEAD_PAD, rows_s).transpose(0, 2, 1)
        oa = _sample_attention(l, q, k, c, cache_kv_latent, cache_k_rope, wl["w_uk"],
                               wl["w_uv"], wl["g_kn"], dec)
        hs = _post(l, hs, sa, oa, gb, ps, wl, 256)
        outs["sc"].append(c.reshape(nb, dec, KV_LORA))
        outs["skr"].append(kr.reshape(nb, dec, QK_ROPE))
        outs["sv"].append(vn.reshape(nb, dec, GM_WIDTH))

    return (hp.reshape(batch, seq, D_MODEL), hs.reshape(nb, dec, D_MODEL),
            jnp.stack(outs["pc"]), jnp.stack(outs["pkr"]), jnp.stack(outs["sc"]),
            jnp.stack(outs["skr"]), jnp.stack(outs["sv"]))
```

```python
import functools
import math

import jax
import jax.numpy as jnp
from jax import lax
from jax.experimental import pallas as pl
from jax.experimental.pallas import tpu as pltpu

D_MODEL = 1024
DEPTH = 2
CHUNK = 64
N_HEADS = 8
QK_NOPE = 128
QK_ROPE = 64
V_DIM = 128
Q_LORA = 384
KV_LORA = 512
GM_CHUNK = 128
GM_WIDTH = 1024
GM_GROUPS = 8
D_FF = 2816
PLE_DIM = 256
ROPE_THETA = 10000.0
EPS = 1e-6
SCALE = (QK_NOPE + QK_ROPE) ** -0.5
Q_SCALE = SCALE * math.log2(math.e)

LANE = 128
HEAD_PAD = 2 * LANE
HALF = QK_ROPE // 2
VMEM_LIMIT = 56 * 1024 * 1024
FF_CHUNK = 256
ROW_TILE = 256
FFN_TILE = 512
ATT_TQ = 512
ATT_SUB = 256
ATT_TK = 1024
ATT_HEADS = 2
ATT_MAX_JUMP = 64.0
SAMPLE_TK = 512
NEG = -1e30

F32 = jnp.float32
BF16 = jnp.bfloat16


def _dot(a, b):
    return jnp.dot(a, b, preferred_element_type=F32)


def _dot_nt(a, b):
    return lax.dot_general(a, b, (((1,), (1,)), ((), ())), preferred_element_type=F32)


def _rms(x, g, n=None):
    n = x.shape[-1] if n is None else n
    ms = jnp.sum(x * x, axis=-1, keepdims=True) * (1.0 / n)
    return x * lax.rsqrt(ms + EPS) * g


def _const_spec(shape):
    nd = len(shape)
    return pl.BlockSpec(shape, lambda *_: (0,) * nd, pipeline_mode=pl.Buffered(1))


def _layer_spec(arr, layer):
    nd = arr.ndim - 1
    return pl.BlockSpec((None,) + arr.shape[1:], lambda *_: (layer,) + (0,) * nd,
                        pipeline_mode=pl.Buffered(1))


def _params(*sem):
    return pltpu.CompilerParams(dimension_semantics=sem, vmem_limit_bytes=VMEM_LIMIT)


def _swiglu_half(x, g, wg_ref, wu_ref, wd_ref):
    n = _rms(x, g).astype(BF16)
    acc = jnp.zeros(x.shape, F32)
    for c in range(D_FF // FF_CHUNK):
        sl = slice(c * FF_CHUNK, (c + 1) * FF_CHUNK)
        gt = _dot(n, wg_ref[:, sl])
        up = _dot(n, wu_ref[:, sl])
        a = (gt * jax.nn.sigmoid(gt) * up).astype(BF16)
        acc = acc + _dot(a, wd_ref[sl, :])
    return x + 0.5 * acc


def _ffn_kernel(x_ref, g_ref, wg_ref, wu_ref, wd_ref, o_ref):
    o_ref[...] = _swiglu_half(x_ref[...], g_ref[...], wg_ref, wu_ref, wd_ref)


def _ffn(layer, x, wp, tm):
    rows = x.shape[0]
    row = pl.BlockSpec((tm, D_MODEL), lambda i: (i, 0))
    consts = [wp["g_f1"], wp["f1_wg"], wp["f1_wu"], wp["f1_wd"]]
    return pl.pallas_call(
        _ffn_kernel,
        grid=(rows // tm,),
        in_specs=[row] + [_layer_spec(a, layer) for a in consts],
        out_specs=row,
        out_shape=jax.ShapeDtypeStruct(x.shape, F32),
        compiler_params=_params("parallel"),
        name="ffn",
    )(x, *consts)


def _rope_rows(x, tab):
    lane = lax.broadcasted_iota(jnp.int32, x.shape, 1)
    swapped = jnp.where(lane < HALF, pltpu.roll(x, LANE - HALF, 1), pltpu.roll(x, HALF, 1))
    return x * tab[:, :LANE] + swapped * tab[:, LANE:]


def _mix_kernel(h_ref, tab_ref, tabt_ref, g_qt, ws_ref, bs_ref,
                g_mix, w_q, w_kv, w_kr, w_u, w_v, w_ga, w_gb,
                g_qa, w_uqt, g_kva, g_kr, w_uk, w_uvt, g_kn, g_v,
                *out_refs, tm, emit_v, emit_vn):
    out_refs = list(out_refs)
    qt_ref, k_ref, c_ref, krt_ref, sa_ref, gb_ref = out_refs[:6]
    rest = out_refs[6:]
    vt_ref = rest.pop(0) if emit_v else None
    vn_ref = rest.pop(0) if emit_vn else None

    n = _rms(h_ref[...], g_mix[...]).astype(BF16)

    kr = _rope_rows(_rms(_dot(n, w_kr[...]), g_kr[...], QK_ROPE), tab_ref[...])
    krt_ref[...] = kr.T[:QK_ROPE]

    c = _rms(_dot(n, w_kv[...]), g_kva[...])
    c_ref[...] = c
    cb = c.astype(BF16)
    kn = _dot(cb, w_uk[...])
    for h in range(N_HEADS):
        kh = _rms(kn[:, h * QK_NOPE:(h + 1) * QK_NOPE], g_kn[...])
        k_ref[h] = jnp.concatenate([kh, kr], axis=1).astype(BF16)
    if emit_v:
        vt = _dot_nt(w_uvt[...], cb)
        for h in range(N_HEADS):
            vt_ref[h, 0] = vt[h * V_DIM:(h + 1) * V_DIM].astype(BF16)

    ql = _rms(_dot(n, w_q[...]), g_qa[...]).astype(BF16)
    qft = _dot_nt(w_uqt[...], ql)
    cos, sin = tabt_ref[:HALF, :], tabt_ref[HALF:, :]
    g_nope = g_qt[:QK_NOPE, :]
    g_r1 = g_qt[QK_NOPE:QK_NOPE + HALF, :]
    g_r2 = g_qt[QK_NOPE + HALF:, :]
    zpad = jnp.zeros((HEAD_PAD - QK_NOPE - QK_ROPE, tm), F32)
    for h in range(N_HEADS):
        blk = qft[h * HEAD_PAD:(h + 1) * HEAD_PAD]
        nope = blk[:QK_NOPE]
        x1 = blk[QK_NOPE:QK_NOPE + HALF]
        x2 = blk[QK_NOPE + HALF:QK_NOPE + QK_ROPE]
        rn = lax.rsqrt(jnp.sum(nope * nope, axis=0, keepdims=True) * (1.0 / QK_NOPE) + EPS)
        rr = lax.rsqrt((jnp.sum(x1 * x1, axis=0, keepdims=True)
                        + jnp.sum(x2 * x2, axis=0, keepdims=True)) * (1.0 / QK_ROPE) + EPS)
        y1 = x1 * rr * g_r1
        y2 = x2 * rr * g_r2
        qh = jnp.concatenate([nope * rn * g_nope, y1 * cos - y2 * sin,
                              y1 * sin + y2 * cos, zpad], axis=0)
        qt_ref[h * HEAD_PAD:(h + 1) * HEAD_PAD, :] = (qh * Q_SCALE).astype(BF16)

    vn = _rms(_dot(n, w_v[...]), g_v[...])
    if emit_vn:
        vn_ref[...] = vn
    vb = vn.astype(BF16)
    u = _dot(n, w_u[...])
    gbz = jax.nn.sigmoid(_dot(n, w_gb[...]))
    nc = tm // GM_CHUNK
    tril = (lax.broadcasted_iota(jnp.int32, (GM_CHUNK, GM_CHUNK), 0)
            >= lax.broadcasted_iota(jnp.int32, (GM_CHUNK, GM_CHUNK), 1))
    for g in range(GM_GROUPS):
        cs = slice(g * LANE, (g + 1) * LANE)
        w = jnp.where(tril, ws_ref[g], 0.0).astype(BF16)
        rhs = jnp.concatenate(
            [vb[c_ * GM_CHUNK:(c_ + 1) * GM_CHUNK, cs] for c_ in range(nc)], axis=1)
        mixed = _dot(w, rhs)
        for c_ in range(nc):
            rs = slice(c_ * GM_CHUNK, (c_ + 1) * GM_CHUNK)
            ob = u[rs, cs] * (mixed[:, c_ * LANE:(c_ + 1) * LANE] + bs_ref[g])
            gb_ref[rs, cs] = gbz[rs, cs] * ob
    sa_ref[...] = jax.nn.sigmoid(_dot(n, w_ga[...]))


def _mix_in(layer, h, tab, tabt, wp, ws_eff, bs_eff, tm, emit_v, emit_vn):
    rows = h.shape[0]
    row = lambda w: pl.BlockSpec((tm, w), lambda i: (i, 0))
    col = lambda r: pl.BlockSpec((r, tm), lambda i: (0, i))
    head = lambda w: pl.BlockSpec((N_HEADS, tm, w), lambda i: (0, i, 0))
    g_qt = jnp.broadcast_to(wp["g_qt"][layer], (QK_NOPE + QK_ROPE, tm))
    consts = [g_qt, ws_eff, bs_eff]
    stacked = [wp[k] for k in ("g_mix", "w_q", "w_kv", "w_kr", "w_u", "w_v", "w_ga", "w_gb",
                               "g_qa", "w_uqt", "g_kva", "g_kr", "w_uk", "w_uvt", "g_kn",
                               "g_v")]
    out_shape = [jax.ShapeDtypeStruct((N_HEADS * HEAD_PAD, rows), BF16),
                 jax.ShapeDtypeStruct((N_HEADS, rows, HEAD_PAD), BF16),
                 jax.ShapeDtypeStruct((rows, KV_LORA), F32),
                 jax.ShapeDtypeStruct((QK_ROPE, rows), F32),
                 jax.ShapeDtypeStruct((rows, D_MODEL), F32),
                 jax.ShapeDtypeStruct((rows, D_MODEL), F32)]
    out_specs = [col(N_HEADS * HEAD_PAD), head(HEAD_PAD), row(KV_LORA), col(QK_ROPE),
                 row(D_MODEL), row(D_MODEL)]
    if emit_v:
        per = ATT_TK // tm
        out_shape.append(jax.ShapeDtypeStruct((N_HEADS, rows // ATT_TK, V_DIM, ATT_TK), BF16))
        out_specs.append(pl.BlockSpec((N_HEADS, 1, V_DIM, tm),
                                      lambda i: (0, i // per, 0, i % per)))
    if emit_vn:
        out_shape.append(jax.ShapeDtypeStruct((rows, GM_WIDTH), F32))
        out_specs.append(row(GM_WIDTH))
    return pl.pallas_call(
        functools.partial(_mix_kernel, tm=tm, emit_v=emit_v, emit_vn=emit_vn),
        grid=(rows // tm,),
        in_specs=[row(D_MODEL), row(2 * LANE), col(QK_ROPE)]
        + [_const_spec(a.shape) for a in consts]
        + [_layer_spec(a, layer) for a in stacked],
        out_specs=out_specs,
        out_shape=out_shape,
        compiler_params=_params("parallel"),
        name="mix_in",
    )(h, tab, tabt, *consts, *stacked)


def _attn_kernel(qt_ref, k_ref, vt_ref, o_ref):
    tq, tk, nh = ATT_TQ, ATT_TK, ATT_HEADS
    i = pl.program_id(1)
    ns, ws = tq // ATT_SUB, ATT_SUB
    chains = [(h, a) for h in range(nh) for a in range(ns)]
    qts = {(h, a): qt_ref[h * HEAD_PAD:(h + 1) * HEAD_PAD, a * ws:(a + 1) * ws]
           for h, a in chains}

    def scores(j, ch):
        start = pl.multiple_of(j * tk, tk)
        return _dot(k_ref[ch[0], pl.ds(start, tk), :], qts[ch])

    n_full = (i * tq) // tk
    dk = (lax.broadcasted_iota(jnp.int32, (tk, ws), 0) // CHUNK
          - lax.broadcasted_iota(jnp.int32, (tk, ws), 1) // CHUNK)
    off = (i * tq - n_full * tk) // CHUNK
    masks = [dk <= off + a * (ws // CHUNK) for a in range(ns)]

    def write(ch, l, acc):
        h, a = ch
        o_ref[a * ws:(a + 1) * ws, h * V_DIM:(h + 1) * V_DIM] = (acc / l).T

    def update(st, state):
        m, l, acc, jump = state
        p = jnp.exp2(st - m)
        m_new = jnp.maximum(m, jnp.max(st, axis=0, keepdims=True))
        l = l + jnp.sum(p, axis=0, keepdims=True)
        return m_new, l, p.astype(BF16), jnp.maximum(jump, m_new - m)

    def body(j, states):
        sts = [scores(j, ch) for ch in chains]
        ups = [update(st, state) for st, state in zip(sts, states)]
        new_states = []
        for ch, state, (m, l, p, jump) in zip(chains, states, ups):
            acc = state[2] + _dot(vt_ref[ch[0], j], p)
            alpha = jnp.exp2(state[0] - m)
            new_states.append((m, alpha * l, alpha * acc, jump))
        return tuple(new_states)

    init = tuple(
        (jnp.max(_dot(k_ref[ch[0], 0:CHUNK, :], qts[ch]), axis=0, keepdims=True),
         jnp.zeros((1, ws), F32), jnp.zeros((V_DIM, ws), F32), jnp.zeros((1, ws), F32))
        for ch in chains)
    states = lax.fori_loop(0, n_full, body, init)
    worst = jnp.zeros((1, ws), F32)
    sts = [jnp.where(masks[ch[1]], scores(n_full, ch), NEG) for ch in chains]
    ups = [update(st, state) for st, state in zip(sts, states)]
    for ch, state, (_, l, p, jump) in zip(chains, states, ups):
        write(ch, l, state[2] + _dot(vt_ref[ch[0], n_full], p))
        worst = jnp.maximum(worst, jump)

    @pl.when(jnp.max(worst) > ATT_MAX_JUMP)
    def _():
        def exact(st, state):
            m, l, acc = state
            m_new = jnp.maximum(m, jnp.max(st, axis=0, keepdims=True))
            alpha = jnp.exp2(m - m_new)
            p = jnp.exp2(st - m_new)
            return m_new, alpha * l + jnp.sum(p, axis=0, keepdims=True), alpha * acc, p

        for ch in chains:
            def ebody(j, state, ch=ch):
                m, l, acc, p = exact(scores(j, ch), state)
                return m, l, acc + _dot(vt_ref[ch[0], j], p.astype(BF16))

            state = lax.fori_loop(0, n_full, ebody, (jnp.full((1, ws), NEG, F32),
                                                    jnp.zeros((1, ws), F32),
                                                    jnp.zeros((V_DIM, ws), F32)))
            st = jnp.where(masks[ch[1]], scores(n_full, ch), NEG)
            _, l, acc, p = exact(st, state)
            write(ch, l, acc + _dot(vt_ref[ch[0], n_full], p.astype(BF16)))


def _prompt_attention(qt, k, vt):
    s = k.shape[1]
    nh = ATT_HEADS
    once = pl.Buffered(1)
    return pl.pallas_call(
        _attn_kernel,
        grid=(N_HEADS // nh, s // ATT_TQ),
        in_specs=[pl.BlockSpec((nh * HEAD_PAD, ATT_TQ), lambda h, i: (h, i)),
                  pl.BlockSpec((nh, s, HEAD_PAD), lambda h, i: (h, 0, 0), pipeline_mode=once),
                  pl.BlockSpec((nh, s // ATT_TK, V_DIM, ATT_TK), lambda h, i: (h, 0, 0, 0),
                               pipeline_mode=once)],
        out_specs=pl.BlockSpec((ATT_TQ, nh * V_DIM), lambda h, i: (i, h)),
        out_shape=jax.ShapeDtypeStruct((s, N_HEADS * V_DIM), F32),
        compiler_params=_params("parallel", "arbitrary"),
        name="prompt_attn",
    )(qt, k, vt)


def _sample_attn_kernel(q_ref, kn_ref, cn_ref, cc_ref, ckrt_ref, w_uk, w_uv, g_kn,
                        o_ref, s_scr, cb_scr, *, dec, past):
    tkc = SAMPLE_TK
    nch = past // tkc
    qn = [q_ref[h][:, :QK_NOPE] for h in range(N_HEADS)]
    qr_all = jnp.concatenate(
        [q_ref[h][:, QK_NOPE:QK_NOPE + QK_ROPE] for h in range(N_HEADS)], axis=0)

    for j in range(nch):
        cb = cc_ref[0, 0, j * tkc:(j + 1) * tkc, :].astype(BF16)
        cb_scr[j] = cb
        kn = _dot(cb, w_uk[...])
        s_rope = _dot(qr_all, ckrt_ref[0, 0, :, j * tkc:(j + 1) * tkc].astype(BF16))
        for h in range(N_HEADS):
            kh = _rms(kn[:, h * QK_NOPE:(h + 1) * QK_NOPE], g_kn[...]).astype(BF16)
            s_scr[j, h * dec:(h + 1) * dec, :] = (
                _dot_nt(qn[h], kh) + s_rope[h * dec:(h + 1) * dec])

    s_new = jnp.concatenate([_dot_nt(q_ref[h], kn_ref[h]) for h in range(N_HEADS)], axis=0)
    m = jnp.max(s_new, axis=-1, keepdims=True)
    for j in range(nch):
        m = jnp.maximum(m, jnp.max(s_scr[j], axis=-1, keepdims=True))
    p_n = jnp.exp2(s_new - m)
    l = jnp.sum(p_n, axis=-1, keepdims=True)
    pc = _dot(p_n.astype(BF16), cn_ref[...].astype(BF16))
    for j in range(nch):
        p_c = jnp.exp2(s_scr[j] - m)
        l = l + jnp.sum(p_c, axis=-1, keepdims=True)
        pc = pc + _dot(p_c.astype(BF16), cb_scr[j])
    pcb = (pc / l).astype(BF16)
    for h in range(N_HEADS):
        o_ref[:, h * V_DIM:(h + 1) * V_DIM] = _dot(
            pcb[h * dec:(h + 1) * dec], w_uv[:, h * V_DIM:(h + 1) * V_DIM])


def _sample_attention(layer, q, k_new, c_new, cache_c, cache_krt, wp, dec):
    _, nb, past, _ = cache_c.shape
    rows = nb * dec
    consts = [wp["w_uk"], wp["w_uv"], wp["g_kn"]]
    return pl.pallas_call(
        functools.partial(_sample_attn_kernel, dec=dec, past=past),
        grid=(nb,),
        in_specs=[pl.BlockSpec((N_HEADS, dec, HEAD_PAD), lambda b: (0, b, 0)),
                  pl.BlockSpec((N_HEADS, dec, HEAD_PAD), lambda b: (0, b, 0)),
                  pl.BlockSpec((dec, KV_LORA), lambda b: (b, 0)),
                  pl.BlockSpec((1, 1, past, KV_LORA), lambda b: (layer, b, 0, 0)),
                  pl.BlockSpec((1, 1, QK_ROPE, past), lambda b: (layer, b, 0, 0))]
        + [_layer_spec(a, layer) for a in consts],
        out_specs=pl.BlockSpec((dec, N_HEADS * V_DIM), lambda b: (b, 0)),
        out_shape=jax.ShapeDtypeStruct((rows, N_HEADS * V_DIM), F32),
        scratch_shapes=[pltpu.VMEM((past // SAMPLE_TK, N_HEADS * dec, SAMPLE_TK), F32),
                        pltpu.VMEM((past // SAMPLE_TK, SAMPLE_TK, KV_LORA), BF16)],
        compiler_params=_params("parallel"),
        name="sample_attn",
    )(q, k_new, c_new, cache_c, cache_krt, *consts)


def _post_kernel(h_ref, sa_ref, oa_ref, gb_ref, p_ref, w_o, g_f2, wg, wu, wd,
                 g_ple, w_pg, w_pe, o_ref):
    mixed = (sa_ref[...] * oa_ref[...] + gb_ref[...]).astype(BF16)
    h = h_ref[...] + _dot(mixed, w_o[...])
    h = _swiglu_half(h, g_f2[...], wg, wu, wd)
    gate = jax.nn.sigmoid(_dot(_rms(h, g_ple[...]).astype(BF16), w_pg[...]))
    o_ref[...] = h + gate * _dot(p_ref[0].astype(BF16), w_pe[...])


def _post(layer, h, sa, oa, gb, p, wp, tm):
    rows = h.shape[0]
    row = lambda w: pl.BlockSpec((tm, w), lambda i: (i, 0))
    consts = [wp[k] for k in ("w_o", "g_f2", "f2_wg", "f2_wu", "f2_wd", "g_ple", "w_pg", "w_pe")]
    return pl.pallas_call(
        _post_kernel,
        grid=(rows // tm,),
        in_specs=[row(D_MODEL)] * 4
        + [pl.BlockSpec((1, tm, PLE_DIM), lambda i: (layer, i, 0))]
        + [_layer_spec(a, layer) for a in consts],
        out_specs=row(D_MODEL),
        out_shape=jax.ShapeDtypeStruct(h.shape, F32),
        compiler_params=_params("parallel"),
        name="post",
    )(h, sa, oa, gb, p, *consts)


def _rope_tables(pos):
    inv = ROPE_THETA ** (-jnp.arange(HALF, dtype=F32) / HALF)
    posf = pos.astype(F32)
    zeros = jnp.zeros((LANE - QK_ROPE,), F32)
    ang = posf[:, None] * jnp.concatenate([inv, inv, zeros])[None, :]
    keep = jnp.concatenate([jnp.ones((QK_ROPE,), F32), zeros])
    sign = jnp.concatenate([-jnp.ones((HALF,), F32), jnp.ones((HALF,), F32), zeros])
    tab = jnp.concatenate([jnp.cos(ang) * keep, jnp.sin(ang) * sign], axis=1)
    ang_t = inv[:, None] * posf[None, :]
    tabt = jnp.concatenate([jnp.cos(ang_t), jnp.sin(ang_t)], axis=0)
    return tab, tabt


def _prep_weights(w):
    row = lambda a: a[:, None, :].astype(F32)
    pad_lane = lambda a: jnp.pad(a, [(0, 0)] * (a.ndim - 1) + [(0, LANE - a.shape[-1])])
    b = lambda a: a.astype(BF16)
    w_in = w["w_in"]
    o_kv = Q_LORA
    o_kr = o_kv + KV_LORA
    o_u = o_kr + QK_ROPE
    o_v = o_u + GM_WIDTH
    o_ga = o_v + GM_WIDTH
    o_gb = o_ga + D_MODEL
    w_uq = w["w_uq"].reshape(DEPTH, Q_LORA, N_HEADS, QK_NOPE + QK_ROPE)
    w_uq = jnp.pad(w_uq, ((0, 0), (0, 0), (0, 0), (0, HEAD_PAD - QK_NOPE - QK_ROPE)))
    w_uq = w_uq.reshape(DEPTH, Q_LORA, N_HEADS * HEAD_PAD)
    w_uv = w["w_uv"].reshape(DEPTH, KV_LORA, N_HEADS * V_DIM)
    g_qt = jnp.concatenate([w["q_nope_norm"], w["q_rope_norm"]], axis=1)[:, :, None]
    return dict(
        g_f1=row(w["ffn1_norm"]), f1_wg=b(w["ffn1_w_gate"]), f1_wu=b(w["ffn1_w_up"]),
        f1_wd=b(w["ffn1_w_down"]),
        g_mix=row(w["mix_norm"]),
        w_q=b(w_in[:, :, :o_kv]), w_kv=b(w_in[:, :, o_kv:o_kr]),
        w_kr=b(pad_lane(w_in[:, :, o_kr:o_u])),
        w_u=b(w_in[:, :, o_u:o_v]), w_v=b(w_in[:, :, o_v:o_ga]),
        w_ga=b(w_in[:, :, o_ga:o_gb]), w_gb=b(w_in[:, :, o_gb:]),
        g_qa=row(w["q_a_norm"]), w_uqt=b(jnp.swapaxes(w_uq, 1, 2)), g_qt=g_qt.astype(F32),
        g_kva=row(w["kv_a_norm"]), g_kr=pad_lane(row(w["k_rope_norm"])),
        w_uk=b(w["w_uk"].reshape(DEPTH, KV_LORA, N_HEADS * QK_NOPE)),
        w_uv=b(w_uv), w_uvt=b(jnp.swapaxes(w_uv, 1, 2)),
        g_kn=row(w["k_nope_norm"]), g_v=row(w["gm_v_norm"]),
        w_o=b(w["w_o"]),
        g_f2=row(w["ffn2_norm"]), f2_wg=b(w["ffn2_w_gate"]), f2_wu=b(w["ffn2_w_up"]),
        f2_wd=b(w["ffn2_w_down"]),
        g_ple=row(w["ple_norm"]), w_pg=b(w["ple_w_gate"]), w_pe=b(w["ple_w_proj"]),
    )


def _gate_operands(w_s, b_s, length):
    reps = GM_CHUNK // length
    eye = jnp.eye(reps, dtype=w_s.dtype)
    ws = jnp.einsum("ab,gts->gatbs", eye, w_s[:, :length, :length])
    ws = ws.reshape(GM_GROUPS, GM_CHUNK, GM_CHUNK)
    bs = jnp.tile(b_s[:, :length], (1, reps))
    return ws, jnp.broadcast_to(bs[:, :, None], (GM_GROUPS, GM_CHUNK, LANE))


def kernel(x_prompt, x_sample, cache_kv_latent, cache_k_rope, p_prompt, p_sample, ffn1_norm, ffn1_w_gate, ffn1_w_up, ffn1_w_down, mix_norm, w_in, q_a_norm, w_uq, q_nope_norm, q_rope_norm, kv_a_norm, k_rope_norm, w_uk, w_uv, k_nope_norm, gm_v_norm, gm_w_s, gm_b_s, w_o, ffn2_norm, ffn2_w_gate, ffn2_w_up, ffn2_w_down, ple_norm, ple_w_gate, ple_w_proj):
    w = dict(ffn1_norm=ffn1_norm, ffn1_w_gate=ffn1_w_gate, ffn1_w_up=ffn1_w_up,
             ffn1_w_down=ffn1_w_down, mix_norm=mix_norm, w_in=w_in, q_a_norm=q_a_norm,
             w_uq=w_uq, q_nope_norm=q_nope_norm, q_rope_norm=q_rope_norm,
             kv_a_norm=kv_a_norm, k_rope_norm=k_rope_norm, w_uk=w_uk, w_uv=w_uv,
             k_nope_norm=k_nope_norm, gm_v_norm=gm_v_norm, w_o=w_o, ffn2_norm=ffn2_norm,
             ffn2_w_gate=ffn2_w_gate, ffn2_w_up=ffn2_w_up, ffn2_w_down=ffn2_w_down,
             ple_norm=ple_norm, ple_w_gate=ple_w_gate, ple_w_proj=ple_w_proj)
    batch, seq, _ = x_prompt.shape
    nb, dec, _ = x_sample.shape
    past = cache_kv_latent.shape[2]
    rows_s = nb * dec
    assert batch == 1 and seq % ATT_TK == 0 and rows_s % ROW_TILE == 0
    assert GM_CHUNK % dec == 0 and past % SAMPLE_TK == 0

    wp = _prep_weights(w)
    tab_p, tabt_p = _rope_tables(jnp.arange(seq, dtype=jnp.int32))
    tab_s, tabt_s = _rope_tables(past + jnp.arange(dec, dtype=jnp.int32))
    tab_s, tabt_s = jnp.tile(tab_s, (nb, 1)), jnp.tile(tabt_s, (1, nb))
    hp = x_prompt.reshape(seq, D_MODEL)
    hs = x_sample.reshape(rows_s, D_MODEL)
    pp = p_prompt.reshape(DEPTH, seq, PLE_DIM)
    ps = p_sample.reshape(DEPTH, rows_s, PLE_DIM)
    cache_krt = jnp.swapaxes(cache_k_rope, 2, 3)

    outs = {k: [] for k in ("pc", "pkr", "sc", "skr", "sv")}
    for l in range(DEPTH):
        ws_p, bs_p = _gate_operands(gm_w_s[l], gm_b_s[l], GM_CHUNK)
        ws_s, bs_s = _gate_operands(gm_w_s[l], gm_b_s[l], dec)

        hp = _ffn(l, hp, wp, FFN_TILE)
        qt, k, c, krt, sa, gb, vt = _mix_in(l, hp, tab_p, tabt_p, wp, ws_p, bs_p,
                                            ROW_TILE, True, False)
        oa = _prompt_attention(qt, k, vt)
        hp = _post(l, hp, sa, oa, gb, pp, wp, FFN_TILE)
        outs["pc"].append(c.reshape(batch, seq, KV_LORA))
        outs["pkr"].append(krt.reshape(batch, QK_ROPE, seq))

        hs = _ffn(l, hs, wp, ROW_TILE)
        qt, k, c, krt, sa, gb, vn = _mix_in(l, hs, tab_s, tabt_s, wp, ws_s, bs_s,
                                            ROW_TILE, False, True)
        q = qt.reshape(N_HEADS, HEAD_PAD, rows_s).transpose(0, 2, 1)
        oa = _sample_attention(l, q, k, c, cache_kv_latent, cache_krt, wp, dec)
        hs = _post(l, hs, sa, oa, gb, ps, wp, ROW_TILE)
        outs["sc"].append(c.reshape(nb, dec, KV_LORA))
        outs["skr"].append(krt.T.reshape(nb, dec, QK_ROPE))
        outs["sv"].append(vn.reshape(nb, dec, GM_WIDTH))

    return (hp.reshape(batch, seq, D_MODEL), hs.reshape(nb, dec, D_MODEL),
            jnp.stack(outs["pc"]), jnp.swapaxes(jnp.stack(outs["pkr"]), 2, 3),
            jnp.stack(outs["sc"]), jnp.stack(outs["skr"]), jnp.stack(outs["sv"]))
```

```python
import functools
import math

import jax
import jax.numpy as jnp
from jax import lax
from jax.experimental import pallas as pl
from jax.experimental.pallas import tpu as pltpu

D_MODEL = 1024
DEPTH = 2
CHUNK = 64
N_HEADS = 8
QK_NOPE = 128
QK_ROPE = 64
V_DIM = 128
Q_LORA = 384
KV_LORA = 512
GM_CHUNK = 128
GM_WIDTH = 1024
GM_GROUPS = 8
D_FF = 2816
PLE_DIM = 256
ROPE_THETA = 10000.0
EPS = 1e-6
SCALE = (QK_NOPE + QK_ROPE) ** -0.5
Q_SCALE = SCALE * math.log2(math.e)

LANE = 128
HEAD_PAD = 2 * LANE
HALF = QK_ROPE // 2
VMEM_LIMIT = 56 * 1024 * 1024
FF_CHUNK = 256
ROW_TILE = 256
FFN_TILE = 512
ATT_TQ = 512
ATT_SUB = 256
ATT_TK = 1024
ATT_HEADS = 2
ATT_MAX_JUMP = 64.0
SAMPLE_TK = 512
NEG = -1e30

F32 = jnp.float32
BF16 = jnp.bfloat16


def _dot(a, b):
    return jnp.dot(a, b, preferred_element_type=F32)


def _dot_nt(a, b):
    return lax.dot_general(a, b, (((1,), (1,)), ((), ())), preferred_element_type=F32)


def _rms(x, g, n=None):
    n = x.shape[-1] if n is None else n
    ms = jnp.sum(x * x, axis=-1, keepdims=True) * (1.0 / n)
    return x * lax.rsqrt(ms + EPS) * g


def _const_spec(shape):
    nd = len(shape)
    return pl.BlockSpec(shape, lambda *_: (0,) * nd, pipeline_mode=pl.Buffered(1))


def _layer_spec(arr, layer):
    nd = arr.ndim - 1
    return pl.BlockSpec((None,) + arr.shape[1:], lambda *_: (layer,) + (0,) * nd,
                        pipeline_mode=pl.Buffered(1))


def _params(*sem):
    return pltpu.CompilerParams(dimension_semantics=sem, vmem_limit_bytes=VMEM_LIMIT)


def _swiglu_half(x, g, wg_ref, wu_ref, wd_ref):
    n = _rms(x, g).astype(BF16)
    acc = jnp.zeros(x.shape, F32)
    for c in range(D_FF // FF_CHUNK):
        sl = slice(c * FF_CHUNK, (c + 1) * FF_CHUNK)
        gt = _dot(n, wg_ref[:, sl])
        up = _dot(n, wu_ref[:, sl])
        a = (gt * jax.nn.sigmoid(gt) * up).astype(BF16)
        acc = acc + _dot(a, wd_ref[sl, :])
    return x + 0.5 * acc


def _ffn_kernel(x_ref, g_ref, wg_ref, wu_ref, wd_ref, o_ref):
    o_ref[...] = _swiglu_half(x_ref[...], g_ref[...], wg_ref, wu_ref, wd_ref)


def _ffn(layer, x, wp, tm):
    rows = x.shape[0]
    row = pl.BlockSpec((tm, D_MODEL), lambda i: (i, 0))
    consts = [wp["g_f1"], wp["f1_wg"], wp["f1_wu"], wp["f1_wd"]]
    return pl.pallas_call(
        _ffn_kernel,
        grid=(rows // tm,),
        in_specs=[row] + [_layer_spec(a, layer) for a in consts],
        out_specs=row,
        out_shape=jax.ShapeDtypeStruct(x.shape, F32),
        compiler_params=_params("parallel"),
        name="ffn",
    )(x, *consts)


def _rope_rows(x, tab):
    lane = lax.broadcasted_iota(jnp.int32, x.shape, 1)
    swapped = jnp.where(lane < HALF, pltpu.roll(x, LANE - HALF, 1), pltpu.roll(x, HALF, 1))
    return x * tab[:, :LANE] + swapped * tab[:, LANE:]


def _mix_kernel(h_ref, tab_ref, tabt_ref, g_qt, ws_ref, bs_ref,
                g_mix, w_q, w_kv, w_kr, w_u, w_v, w_ga, w_gb,
                g_qa, w_uqt, g_kva, g_kr, w_uk, w_uvt, g_kn, g_v,
                *out_refs, tm, emit_v, emit_vn):
    out_refs = list(out_refs)
    qt_ref, k_ref, c_ref, krt_ref, sa_ref, gb_ref = out_refs[:6]
    rest = out_refs[6:]
    vt_ref = rest.pop(0) if emit_v else None
    vn_ref = rest.pop(0) if emit_vn else None

    n = _rms(h_ref[...], g_mix[...]).astype(BF16)

    kr = _rope_rows(_rms(_dot(n, w_kr[...]), g_kr[...], QK_ROPE), tab_ref[...])
    krt_ref[...] = kr.T[:QK_ROPE]

    c = _rms(_dot(n, w_kv[...]), g_kva[...])
    c_ref[...] = c
    cb = c.astype(BF16)
    kn = _dot(cb, w_uk[...])
    for h in range(N_HEADS):
        kh = _rms(kn[:, h * QK_NOPE:(h + 1) * QK_NOPE], g_kn[...])
        k_ref[h] = jnp.concatenate([kh, kr], axis=1).astype(BF16)
    if emit_v:
        vt = _dot_nt(w_uvt[...], cb)
        for h in range(N_HEADS):
            vt_ref[h, 0] = vt[h * V_DIM:(h + 1) * V_DIM].astype(BF16)

    ql = _rms(_dot(n, w_q[...]), g_qa[...]).astype(BF16)
    qft = _dot_nt(w_uqt[...], ql)
    cos, sin = tabt_ref[:HALF, :], tabt_ref[HALF:, :]
    g_nope = g_qt[:QK_NOPE, :]
    g_r1 = g_qt[QK_NOPE:QK_NOPE + HALF, :]
    g_r2 = g_qt[QK_NOPE + HALF:, :]
    zpad = jnp.zeros((HEAD_PAD - QK_NOPE - QK_ROPE, tm), F32)
    for h in range(N_HEADS):
        blk = qft[h * HEAD_PAD:(h + 1) * HEAD_PAD]
        nope = blk[:QK_NOPE]
        x1 = blk[QK_NOPE:QK_NOPE + HALF]
        x2 = blk[QK_NOPE + HALF:QK_NOPE + QK_ROPE]
        rn = lax.rsqrt(jnp.sum(nope * nope, axis=0, keepdims=True) * (1.0 / QK_NOPE) + EPS)
        rr = lax.rsqrt((jnp.sum(x1 * x1, axis=0, keepdims=True)
                        + jnp.sum(x2 * x2, axis=0, keepdims=True)) * (1.0 / QK_ROPE) + EPS)
        y1 = x1 * rr * g_r1
        y2 = x2 * rr * g_r2
        qh = jnp.concatenate([nope * rn * g_nope, y1 * cos - y2 * sin,
                              y1 * sin + y2 * cos, zpad], axis=0)
        qt_ref[h * HEAD_PAD:(h + 1) * HEAD_PAD, :] = (qh * Q_SCALE).astype(BF16)

    vn = _rms(_dot(n, w_v[...]), g_v[...])
    if emit_vn:
        vn_ref[...] = vn
    vb = vn.astype(BF16)
    u = _dot(n, w_u[...])
    gbz = jax.nn.sigmoid(_dot(n, w_gb[...]))
    nc = tm // GM_CHUNK
    tril = (lax.broadcasted_iota(jnp.int32, (GM_CHUNK, GM_CHUNK), 0)
            >= lax.broadcasted_iota(jnp.int32, (GM_CHUNK, GM_CHUNK), 1))
    for g in range(GM_GROUPS):
        cs = slice(g * LANE, (g + 1) * LANE)
        w = jnp.where(tril, ws_ref[g], 0.0).astype(BF16)
        rhs = jnp.concatenate(
            [vb[c_ * GM_CHUNK:(c_ + 1) * GM_CHUNK, cs] for c_ in range(nc)], axis=1)
        mixed = _dot(w, rhs)
        for c_ in range(nc):
            rs = slice(c_ * GM_CHUNK, (c_ + 1) * GM_CHUNK)
            ob = u[rs, cs] * (mixed[:, c_ * LANE:(c_ + 1) * LANE] + bs_ref[g])
            gb_ref[rs, cs] = gbz[rs, cs] * ob
    sa_ref[...] = jax.nn.sigmoid(_dot(n, w_ga[...]))


def _mix_in(layer, h, tab, tabt, wp, ws_eff, bs_eff, tm, emit_v, emit_vn):
    rows = h.shape[0]
    row = lambda w: pl.BlockSpec((tm, w), lambda i: (i, 0))
    col = lambda r: pl.BlockSpec((r, tm), lambda i: (0, i))
    head = lambda w: pl.BlockSpec((N_HEADS, tm, w), lambda i: (0, i, 0))
    g_qt = jnp.broadcast_to(wp["g_qt"][layer], (QK_NOPE + QK_ROPE, tm))
    consts = [g_qt, ws_eff, bs_eff]
    stacked = [wp[k] for k in ("g_mix", "w_q", "w_kv", "w_kr", "w_u", "w_v", "w_ga", "w_gb",
                               "g_qa", "w_uqt", "g_kva", "g_kr", "w_uk", "w_uvt", "g_kn",
                               "g_v")]
    out_shape = [jax.ShapeDtypeStruct((N_HEADS * HEAD_PAD, rows), BF16),
                 jax.ShapeDtypeStruct((N_HEADS, rows, HEAD_PAD), BF16),
                 jax.ShapeDtypeStruct((rows, KV_LORA), F32),
                 jax.ShapeDtypeStruct((QK_ROPE, rows), F32),
                 jax.ShapeDtypeStruct((rows, D_MODEL), F32),
                 jax.ShapeDtypeStruct((rows, D_MODEL), F32)]
    out_specs = [col(N_HEADS * HEAD_PAD), head(HEAD_PAD), row(KV_LORA), col(QK_ROPE),
                 row(D_MODEL), row(D_MODEL)]
    if emit_v:
        per = ATT_TK // tm
        out_shape.append(jax.ShapeDtypeStruct((N_HEADS, rows // ATT_TK, V_DIM, ATT_TK), BF16))
        out_specs.append(pl.BlockSpec((N_HEADS, 1, V_DIM, tm),
                                      lambda i: (0, i // per, 0, i % per)))
    if emit_vn:
        out_shape.append(jax.ShapeDtypeStruct((rows, GM_WIDTH), F32))
        out_specs.append(row(GM_WIDTH))
    return pl.pallas_call(
        functools.partial(_mix_kernel, tm=tm, emit_v=emit_v, emit_vn=emit_vn),
        grid=(rows // tm,),
        in_specs=[row(D_MODEL), row(2 * LANE), col(QK_ROPE)]
        + [_const_spec(a.shape) for a in consts]
        + [_layer_spec(a, layer) for a in stacked],
        out_specs=out_specs,
        out_shape=out_shape,
        compiler_params=_params("parallel"),
        name="mix_in",
    )(h, tab, tabt, *consts, *stacked)


def _attn_kernel(qt_ref, k_ref, vt_ref, o_ref):
    tq, tk, nh = ATT_TQ, ATT_TK, ATT_HEADS
    i = pl.program_id(1)
    ns, ws = tq // ATT_SUB, ATT_SUB
    chains = [(h, a) for h in range(nh) for a in range(ns)]
    qts = {(h, a): qt_ref[h * HEAD_PAD:(h + 1) * HEAD_PAD, a * ws:(a + 1) * ws]
           for h, a in chains}

    def scores(j, ch):
        start = pl.multiple_of(j * tk, tk)
        return _dot(k_ref[ch[0], pl.ds(start, tk), :], qts[ch])

    n_full = (i * tq) // tk
    dk = (lax.broadcasted_iota(jnp.int32, (tk, ws), 0) // CHUNK
          - lax.broadcasted_iota(jnp.int32, (tk, ws), 1) // CHUNK)
    off = (i * tq - n_full * tk) // CHUNK
    masks = [dk <= off + a * (ws // CHUNK) for a in range(ns)]

    def write(ch, l, acc):
        h, a = ch
        o_ref[a * ws:(a + 1) * ws, h * V_DIM:(h + 1) * V_DIM] = (acc / l).T

    def update(st, m, l, jump):
        p = jnp.exp2(st - m)
        m_new = jnp.maximum(m, jnp.max(st, axis=0, keepdims=True))
        alpha = jnp.exp2(m - m_new)
        l = alpha * (l + jnp.sum(p, axis=0, keepdims=True))
        return m_new, l, jnp.maximum(jump, m_new - m), alpha, p.astype(BF16)

    def body1(j, states):
        sts = [scores(j, ch) for ch in chains]
        ups = [update(st, m, l, jump) for st, (m, l, _, jump) in zip(sts, states)]
        new_states = []
        for ch, state, (m, l, jump, alpha, p) in zip(chains, states, ups):
            acc = alpha * (state[2] + _dot(vt_ref[ch[0], j], p))
            new_states.append((m, l, acc, jump))
        return tuple(new_states)

    def body2(jj, states):
        j0, j1 = 2 * jj, 2 * jj + 1
        sts0 = [scores(j0, ch) for ch in chains]
        sts1 = [scores(j1, ch) for ch in chains]
        ups0 = [update(st, m, l, jump) for st, (m, l, _, jump) in zip(sts0, states)]
        ups1 = [update(st, m, l, jump) for st, (m, l, jump, _, _) in zip(sts1, ups0)]
        pvs0 = [_dot(vt_ref[ch[0], j0], up[4]) for ch, up in zip(chains, ups0)]
        pvs1 = [_dot(vt_ref[ch[0], j1], up[4]) for ch, up in zip(chains, ups1)]
        new_states = []
        for state, u0, u1, pv0, pv1 in zip(states, ups0, ups1, pvs0, pvs1):
            acc = u1[3] * (u0[3] * (state[2] + pv0) + pv1)
            new_states.append((u1[0], u1[1], acc, u1[2]))
        return tuple(new_states)

    init = tuple(
        (jnp.max(_dot(k_ref[ch[0], 0:CHUNK, :], qts[ch]), axis=0, keepdims=True),
         jnp.zeros((1, ws), F32), jnp.zeros((V_DIM, ws), F32), jnp.zeros((1, ws), F32))
        for ch in chains)
    states = lax.fori_loop(0, n_full // 2, body2, init)
    states = lax.cond(n_full % 2 == 1, lambda s: body1(n_full - 1, s), lambda s: s, states)
    worst = jnp.zeros((1, ws), F32)
    sts = [jnp.where(masks[ch[1]], scores(n_full, ch), NEG) for ch in chains]
    ups = [update(st, m, l, jump) for st, (m, l, _, jump) in zip(sts, states)]
    for ch, state, (_, l, jump, alpha, p) in zip(chains, states, ups):
        write(ch, l, alpha * (state[2] + _dot(vt_ref[ch[0], n_full], p)))
        worst = jnp.maximum(worst, jump)

    @pl.when(jnp.max(worst) > ATT_MAX_JUMP)
    def _():
        def exact(st, state):
            m, l, acc = state
            m_new = jnp.maximum(m, jnp.max(st, axis=0, keepdims=True))
            alpha = jnp.exp2(m - m_new)
            p = jnp.exp2(st - m_new)
            return m_new, alpha * l + jnp.sum(p, axis=0, keepdims=True), alpha * acc, p

        for ch in chains:
            def ebody(j, state, ch=ch):
                m, l, acc, p = exact(scores(j, ch), state)
                return m, l, acc + _dot(vt_ref[ch[0], j], p.astype(BF16))

            state = lax.fori_loop(0, n_full, ebody, (jnp.full((1, ws), NEG, F32),
                                                    jnp.zeros((1, ws), F32),
                                                    jnp.zeros((V_DIM, ws), F32)))
            st = jnp.where(masks[ch[1]], scores(n_full, ch), NEG)
            _, l, acc, p = exact(st, state)
            write(ch, l, acc + _dot(vt_ref[ch[0], n_full], p.astype(BF16)))


def _prompt_attention(qt, k, vt):
    s = k.shape[1]
    nh = ATT_HEADS
    once = pl.Buffered(1)
    return pl.pallas_call(
        _attn_kernel,
        grid=(N_HEADS // nh, s // ATT_TQ),
        in_specs=[pl.BlockSpec((nh * HEAD_PAD, ATT_TQ), lambda h, i: (h, i)),
                  pl.BlockSpec((nh, s, HEAD_PAD), lambda h, i: (h, 0, 0), pipeline_mode=once),
                  pl.BlockSpec((nh, s // ATT_TK, V_DIM, ATT_TK), lambda h, i: (h, 0, 0, 0),
                               pipeline_mode=once)],
        out_specs=pl.BlockSpec((ATT_TQ, nh * V_DIM), lambda h, i: (i, h)),
        out_shape=jax.ShapeDtypeStruct((s, N_HEADS * V_DIM), F32),
        compiler_params=_params("parallel", "arbitrary"),
        name="prompt_attn",
    )(qt, k, vt)


def _sample_attn_kernel(q_ref, kn_ref, cn_ref, cc_ref, ckrt_ref, w_ukt, w_uv, g_kn,
                        o_ref, s_scr, cb_scr, *, dec, past):
    tkc = SAMPLE_TK
    nch = past // tkc
    qg = [(q_ref[h][:, :QK_NOPE].astype(F32) * g_kn[...]).astype(BF16) for h in range(N_HEADS)]
    qr_all = jnp.concatenate(
        [q_ref[h][:, QK_NOPE:QK_NOPE + QK_ROPE] for h in range(N_HEADS)], axis=0)

    for j in range(nch):
        cb = cc_ref[0, 0, j * tkc:(j + 1) * tkc, :].astype(BF16)
        cb_scr[j] = cb
        knt = _dot_nt(w_ukt[...], cb)
        s_rope = _dot(qr_all, ckrt_ref[0, 0, :, j * tkc:(j + 1) * tkc].astype(BF16))
        for h in range(N_HEADS):
            kh = knt[h * QK_NOPE:(h + 1) * QK_NOPE]
            r = lax.rsqrt(jnp.sum(kh * kh, axis=0, keepdims=True) * (1.0 / QK_NOPE) + EPS)
            s_scr[j, h * dec:(h + 1) * dec, :] = (
                _dot(qg[h], kh.astype(BF16)) * r + s_rope[h * dec:(h + 1) * dec])

    s_new = jnp.concatenate([_dot_nt(q_ref[h], kn_ref[h]) for h in range(N_HEADS)], axis=0)
    m = jnp.max(s_new, axis=-1, keepdims=True)
    for j in range(nch):
        m = jnp.maximum(m, jnp.max(s_scr[j], axis=-1, keepdims=True))
    p_n = jnp.exp2(s_new - m)
    l = jnp.sum(p_n, axis=-1, keepdims=True)
    pc = _dot(p_n.astype(BF16), cn_ref[...].astype(BF16))
    for j in range(nch):
        p_c = jnp.exp2(s_scr[j] - m)
        l = l + jnp.sum(p_c, axis=-1, keepdims=True)
        pc = pc + _dot(p_c.astype(BF16), cb_scr[j])
    pcb = (pc / l).astype(BF16)
    for h in range(N_HEADS):
        o_ref[:, h * V_DIM:(h + 1) * V_DIM] = _dot(
            pcb[h * dec:(h + 1) * dec], w_uv[:, h * V_DIM:(h + 1) * V_DIM])


def _sample_attention(layer, q, k_new, c_new, cache_c, cache_krt, wp, dec):
    _, nb, past, _ = cache_c.shape
    rows = nb * dec
    consts = [wp["w_ukt"], wp["w_uv"], wp["g_kn"]]
    return pl.pallas_call(
        functools.partial(_sample_attn_kernel, dec=dec, past=past),
        grid=(nb,),
        in_specs=[pl.BlockSpec((N_HEADS, dec, HEAD_PAD), lambda b: (0, b, 0)),
                  pl.BlockSpec((N_HEADS, dec, HEAD_PAD), lambda b: (0, b, 0)),
                  pl.BlockSpec((dec, KV_LORA), lambda b: (b, 0)),
                  pl.BlockSpec((1, 1, past, KV_LORA), lambda b: (layer, b, 0, 0)),
                  pl.BlockSpec((1, 1, QK_ROPE, past), lambda b: (layer, b, 0, 0))]
        + [_layer_spec(a, layer) for a in consts],
        out_specs=pl.BlockSpec((dec, N_HEADS * V_DIM), lambda b: (b, 0)),
        out_shape=jax.ShapeDtypeStruct((rows, N_HEADS * V_DIM), F32),
        scratch_shapes=[pltpu.VMEM((past // SAMPLE_TK, N_HEADS * dec, SAMPLE_TK), F32),
                        pltpu.VMEM((past // SAMPLE_TK, SAMPLE_TK, KV_LORA), BF16)],
        compiler_params=_params("parallel"),
        name="sample_attn",
    )(q, k_new, c_new, cache_c, cache_krt, *consts)


def _post_kernel(h_ref, sa_ref, oa_ref, gb_ref, p_ref, w_o, g_f2, wg, wu, wd,
                 g_ple, w_pg, w_pe, o_ref):
    mixed = (sa_ref[...] * oa_ref[...] + gb_ref[...]).astype(BF16)
    h = h_ref[...] + _dot(mixed, w_o[...])
    h = _swiglu_half(h, g_f2[...], wg, wu, wd)
    gate = jax.nn.sigmoid(_dot(_rms(h, g_ple[...]).astype(BF16), w_pg[...]))
    o_ref[...] = h + gate * _dot(p_ref[0].astype(BF16), w_pe[...])


def _post(layer, h, sa, oa, gb, p, wp, tm):
    rows = h.shape[0]
    row = lambda w: pl.BlockSpec((tm, w), lambda i: (i, 0))
    consts = [wp[k] for k in ("w_o", "g_f2", "f2_wg", "f2_wu", "f2_wd", "g_ple", "w_pg", "w_pe")]
    return pl.pallas_call(
        _post_kernel,
        grid=(rows // tm,),
        in_specs=[row(D_MODEL)] * 4
        + [pl.BlockSpec((1, tm, PLE_DIM), lambda i: (layer, i, 0))]
        + [_layer_spec(a, layer) for a in consts],
        out_specs=row(D_MODEL),
        out_shape=jax.ShapeDtypeStruct(h.shape, F32),
        compiler_params=_params("parallel"),
        name="post",
    )(h, sa, oa, gb, p, *consts)


def _rope_tables(pos):
    inv = ROPE_THETA ** (-jnp.arange(HALF, dtype=F32) / HALF)
    posf = pos.astype(F32)
    zeros = jnp.zeros((LANE - QK_ROPE,), F32)
    ang = posf[:, None] * jnp.concatenate([inv, inv, zeros])[None, :]
    keep = jnp.concatenate([jnp.ones((QK_ROPE,), F32), zeros])
    sign = jnp.concatenate([-jnp.ones((HALF,), F32), jnp.ones((HALF,), F32), zeros])
    tab = jnp.concatenate([jnp.cos(ang) * keep, jnp.sin(ang) * sign], axis=1)
    ang_t = inv[:, None] * posf[None, :]
    tabt = jnp.concatenate([jnp.cos(ang_t), jnp.sin(ang_t)], axis=0)
    return tab, tabt


def _prep_weights(w):
    row = lambda a: a[:, None, :].astype(F32)
    pad_lane = lambda a: jnp.pad(a, [(0, 0)] * (a.ndim - 1) + [(0, LANE - a.shape[-1])])
    b = lambda a: a.astype(BF16)
    w_in = w["w_in"]
    o_kv = Q_LORA
    o_kr = o_kv + KV_LORA
    o_u = o_kr + QK_ROPE
    o_v = o_u + GM_WIDTH
    o_ga = o_v + GM_WIDTH
    o_gb = o_ga + D_MODEL
    w_uq = w["w_uq"].reshape(DEPTH, Q_LORA, N_HEADS, QK_NOPE + QK_ROPE)
    w_uq = jnp.pad(w_uq, ((0, 0), (0, 0), (0, 0), (0, HEAD_PAD - QK_NOPE - QK_ROPE)))
    w_uq = w_uq.reshape(DEPTH, Q_LORA, N_HEADS * HEAD_PAD)
    w_uv = w["w_uv"].reshape(DEPTH, KV_LORA, N_HEADS * V_DIM)
    w_uk = w["w_uk"].reshape(DEPTH, KV_LORA, N_HEADS * QK_NOPE)
    g_qt = jnp.concatenate([w["q_nope_norm"], w["q_rope_norm"]], axis=1)[:, :, None]
    return dict(
        g_f1=row(w["ffn1_norm"]), f1_wg=b(w["ffn1_w_gate"]), f1_wu=b(w["ffn1_w_up"]),
        f1_wd=b(w["ffn1_w_down"]),
        g_mix=row(w["mix_norm"]),
        w_q=b(w_in[:, :, :o_kv]), w_kv=b(w_in[:, :, o_kv:o_kr]),
        w_kr=b(pad_lane(w_in[:, :, o_kr:o_u])),
        w_u=b(w_in[:, :, o_u:o_v]), w_v=b(w_in[:, :, o_v:o_ga]),
        w_ga=b(w_in[:, :, o_ga:o_gb]), w_gb=b(w_in[:, :, o_gb:]),
        g_qa=row(w["q_a_norm"]), w_uqt=b(jnp.swapaxes(w_uq, 1, 2)), g_qt=g_qt.astype(F32),
        g_kva=row(w["kv_a_norm"]), g_kr=pad_lane(row(w["k_rope_norm"])),
        w_uk=b(w_uk), w_ukt=b(jnp.swapaxes(w_uk, 1, 2)),
        w_uv=b(w_uv), w_uvt=b(jnp.swapaxes(w_uv, 1, 2)),
        g_kn=row(w["k_nope_norm"]), g_v=row(w["gm_v_norm"]),
        w_o=b(w["w_o"]),
        g_f2=row(w["ffn2_norm"]), f2_wg=b(w["ffn2_w_gate"]), f2_wu=b(w["ffn2_w_up"]),
        f2_wd=b(w["ffn2_w_down"]),
        g_ple=row(w["ple_norm"]), w_pg=b(w["ple_w_gate"]), w_pe=b(w["ple_w_proj"]),
    )


def _gate_operands(w_s, b_s, length):
    reps = GM_CHUNK // length
    eye = jnp.eye(reps, dtype=w_s.dtype)
    ws = jnp.einsum("ab,gts->gatbs", eye, w_s[:, :length, :length])
    ws = ws.reshape(GM_GROUPS, GM_CHUNK, GM_CHUNK)
    bs = jnp.tile(b_s[:, :length], (1, reps))
    return ws, jnp.broadcast_to(bs[:, :, None], (GM_GROUPS, GM_CHUNK, LANE))


def kernel(x_prompt, x_sample, cache_kv_latent, cache_k_rope, p_prompt, p_sample, ffn1_norm, ffn1_w_gate, ffn1_w_up, ffn1_w_down, mix_norm, w_in, q_a_norm, w_uq, q_nope_norm, q_rope_norm, kv_a_norm, k_rope_norm, w_uk, w_uv, k_nope_norm, gm_v_norm, gm_w_s, gm_b_s, w_o, ffn2_norm, ffn2_w_gate, ffn2_w_up, ffn2_w_down, ple_norm, ple_w_gate, ple_w_proj):
    w = dict(ffn1_norm=ffn1_norm, ffn1_w_gate=ffn1_w_gate, ffn1_w_up=ffn1_w_up,
             ffn1_w_down=ffn1_w_down, mix_norm=mix_norm, w_in=w_in, q_a_norm=q_a_norm,
             w_uq=w_uq, q_nope_norm=q_nope_norm, q_rope_norm=q_rope_norm,
             kv_a_norm=kv_a_norm, k_rope_norm=k_rope_norm, w_uk=w_uk, w_uv=w_uv,
             k_nope_norm=k_nope_norm, gm_v_norm=gm_v_norm, w_o=w_o, ffn2_norm=ffn2_norm,
             ffn2_w_gate=ffn2_w_gate, ffn2_w_up=ffn2_w_up, ffn2_w_down=ffn2_w_down,
             ple_norm=ple_norm, ple_w_gate=ple_w_gate, ple_w_proj=ple_w_proj)
    batch, seq, _ = x_prompt.shape
    nb, dec, _ = x_sample.shape
    past = cache_kv_latent.shape[2]
    rows_s = nb * dec
    assert batch == 1 and seq % ATT_TK == 0 and rows_s % ROW_TILE == 0
    assert GM_CHUNK % dec == 0 and past % SAMPLE_TK == 0

    wp = _prep_weights(w)
    tab_p, tabt_p = _rope_tables(jnp.arange(seq, dtype=jnp.int32))
    tab_s, tabt_s = _rope_tables(past + jnp.arange(dec, dtype=jnp.int32))
    tab_s, tabt_s = jnp.tile(tab_s, (nb, 1)), jnp.tile(tabt_s, (1, nb))
    hp = x_prompt.reshape(seq, D_MODEL)
    hs = x_sample.reshape(rows_s, D_MODEL)
    pp = p_prompt.reshape(DEPTH, seq, PLE_DIM)
    ps = p_sample.reshape(DEPTH, rows_s, PLE_DIM)
    cache_krt = jnp.swapaxes(cache_k_rope, 2, 3)

    outs = {k: [] for k in ("pc", "pkr", "sc", "skr", "sv")}
    for l in range(DEPTH):
        ws_p, bs_p = _gate_operands(gm_w_s[l], gm_b_s[l], GM_CHUNK)
        ws_s, bs_s = _gate_operands(gm_w_s[l], gm_b_s[l], dec)

        hp = _ffn(l, hp, wp, FFN_TILE)
        qt, k, c, krt, sa, gb, vt = _mix_in(l, hp, tab_p, tabt_p, wp, ws_p, bs_p,
                                            ROW_TILE, True, False)
        oa = _prompt_attention(qt, k, vt)
        hp = _post(l, hp, sa, oa, gb, pp, wp, FFN_TILE)
        outs["pc"].append(c.reshape(batch, seq, KV_LORA))
        outs["pkr"].append(krt.reshape(batch, QK_ROPE, seq))

        hs = _ffn(l, hs, wp, ROW_TILE)
        qt, k, c, krt, sa, gb, vn = _mix_in(l, hs, tab_s, tabt_s, wp, ws_s, bs_s,
                                            ROW_TILE, False, True)
        q = qt.reshape(N_HEADS, HEAD_PAD, rows_s).transpose(0, 2, 1)
        oa = _sample_attention(l, q, k, c, cache_kv_latent, cache_krt, wp, dec)
        hs = _post(l, hs, sa, oa, gb, ps, wp, ROW_TILE)
        outs["sc"].append(c.reshape(nb, dec, KV_LORA))
        outs["skr"].append(krt.T.reshape(nb, dec, QK_ROPE))
        outs["sv"].append(vn.reshape(nb, dec, GM_WIDTH))

    return (hp.reshape(batch, seq, D_MODEL), hs.reshape(nb, dec, D_MODEL),
            jnp.stack(outs["pc"]), jnp.swapaxes(jnp.stack(outs["pkr"]), 2, 3),
            jnp.stack(outs["sc"]), jnp.stack(outs["skr"]), jnp.stack(outs["sv"]))
```

```python
import functools
import math

import jax
import jax.numpy as jnp
from jax import lax
from jax.experimental import pallas as pl
from jax.experimental.pallas import tpu as pltpu

D_MODEL = 1024
DEPTH = 2
CHUNK = 64
N_HEADS = 8
QK_NOPE = 128
QK_ROPE = 64
V_DIM = 128
Q_LORA = 384
KV_LORA = 512
GM_CHUNK = 128
GM_WIDTH = 1024
GM_GROUPS = 8
D_FF = 2816
PLE_DIM = 256
ROPE_THETA = 10000.0
EPS = 1e-6
SCALE = (QK_NOPE + QK_ROPE) ** -0.5
Q_SCALE = SCALE * math.log2(math.e)

LANE = 128
HEAD_PAD = 2 * LANE
HALF = QK_ROPE // 2
VMEM_LIMIT = 56 * 1024 * 1024
FF_CHUNK = 256
ROW_TILE = 256
FFN_TILE = 512
ATT_TQ = 1024
ATT_SUB = 256
ATT_TK = 1024
ATT_HEADS = 2
ATT_UNROLL = 2
ATT_MAX_JUMP = 64.0
SAMPLE_TK = 512
NEG = -1e30

F32 = jnp.float32
BF16 = jnp.bfloat16


def _dot(a, b):
    return jnp.dot(a, b, preferred_element_type=F32)


def _dot_nt(a, b):
    return lax.dot_general(a, b, (((1,), (1,)), ((), ())), preferred_element_type=F32)


def _rms(x, g, n=None):
    n = x.shape[-1] if n is None else n
    ms = jnp.sum(x * x, axis=-1, keepdims=True) * (1.0 / n)
    return x * lax.rsqrt(ms + EPS) * g


def _const_spec(shape):
    nd = len(shape)
    return pl.BlockSpec(shape, lambda *_: (0,) * nd, pipeline_mode=pl.Buffered(1))


def _layer_spec(arr, layer):
    nd = arr.ndim - 1
    return pl.BlockSpec((None,) + arr.shape[1:], lambda *_: (layer,) + (0,) * nd,
                        pipeline_mode=pl.Buffered(1))


def _params(*sem):
    return pltpu.CompilerParams(dimension_semantics=sem, vmem_limit_bytes=VMEM_LIMIT)


def _swiglu_half(x, g, wg_ref, wu_ref, wd_ref):
    n = _rms(x, g).astype(BF16)
    acc = jnp.zeros(x.shape, F32)
    for c in range(D_FF // FF_CHUNK):
        sl = slice(c * FF_CHUNK, (c + 1) * FF_CHUNK)
        gt = _dot(n, wg_ref[:, sl])
        up = _dot(n, wu_ref[:, sl])
        a = (gt * jax.nn.sigmoid(gt) * up).astype(BF16)
        acc = acc + _dot(a, wd_ref[sl, :])
    return x + 0.5 * acc


def _ffn_kernel(x_ref, g_ref, wg_ref, wu_ref, wd_ref, o_ref):
    o_ref[...] = _swiglu_half(x_ref[...], g_ref[...], wg_ref, wu_ref, wd_ref)


def _ffn(layer, x, wp, tm):
    rows = x.shape[0]
    row = pl.BlockSpec((tm, D_MODEL), lambda i: (i, 0))
    consts = [wp["g_f1"], wp["f1_wg"], wp["f1_wu"], wp["f1_wd"]]
    return pl.pallas_call(
        _ffn_kernel,
        grid=(rows // tm,),
        in_specs=[row] + [_layer_spec(a, layer) for a in consts],
        out_specs=row,
        out_shape=jax.ShapeDtypeStruct(x.shape, F32),
        compiler_params=_params("parallel"),
        name="ffn",
    )(x, *consts)


def _rope_rows(x, tab):
    lane = lax.broadcasted_iota(jnp.int32, x.shape, 1)
    swapped = jnp.where(lane < HALF, pltpu.roll(x, LANE - HALF, 1), pltpu.roll(x, HALF, 1))
    return x * tab[:, :LANE] + swapped * tab[:, LANE:]


def _mix_kernel(h_ref, tab_ref, tabt_ref, g_qt, ws_ref, bs_ref,
                g_mix, w_q, w_kv, w_kr, w_u, w_v, w_ga, w_gb,
                g_qa, w_uqt, g_kva, g_kr, w_uk, w_uvt, g_kn, g_v,
                *out_refs, tm, emit_v, emit_vn):
    out_refs = list(out_refs)
    qt_ref, k_ref, c_ref, krt_ref, sa_ref, gb_ref = out_refs[:6]
    rest = out_refs[6:]
    vt_ref = rest.pop(0) if emit_v else None
    vn_ref = rest.pop(0) if emit_vn else None

    n = _rms(h_ref[...], g_mix[...]).astype(BF16)

    kv_pre = _dot(n, w_kv[...])
    kr_pre = _dot(n, w_kr[...])
    u = _dot(n, w_u[...])

    c = _rms(kv_pre, g_kva[...])
    c_ref[...] = c
    cb = c.astype(BF16)
    kn = _dot(cb, w_uk[...])
    q_pre = _dot(n, w_q[...])
    gbz = jax.nn.sigmoid(_dot(n, w_gb[...]))

    kr = _rope_rows(_rms(kr_pre, g_kr[...], QK_ROPE), tab_ref[...])
    krt_ref[...] = kr.T[:QK_ROPE]
    for h in range(N_HEADS):
        kh = _rms(kn[:, h * QK_NOPE:(h + 1) * QK_NOPE], g_kn[...])
        k_ref[h] = jnp.concatenate([kh, kr], axis=1).astype(BF16)

    ql = _rms(q_pre, g_qa[...]).astype(BF16)
    qft = _dot_nt(w_uqt[...], ql)
    if emit_v:
        vt = _dot_nt(w_uvt[...], cb)
        for h in range(N_HEADS):
            vt_ref[h, 0] = vt[h * V_DIM:(h + 1) * V_DIM].astype(BF16)
    v_pre = _dot(n, w_v[...])
    sa_ref[...] = jax.nn.sigmoid(_dot(n, w_ga[...]))
    cos, sin = tabt_ref[:HALF, :], tabt_ref[HALF:, :]
    g_nope = g_qt[:QK_NOPE, :]
    g_r1 = g_qt[QK_NOPE:QK_NOPE + HALF, :]
    g_r2 = g_qt[QK_NOPE + HALF:, :]
    zpad = jnp.zeros((HEAD_PAD - QK_NOPE - QK_ROPE, tm), F32)
    for h in range(N_HEADS):
        blk = qft[h * HEAD_PAD:(h + 1) * HEAD_PAD]
        nope = blk[:QK_NOPE]
        x1 = blk[QK_NOPE:QK_NOPE + HALF]
        x2 = blk[QK_NOPE + HALF:QK_NOPE + QK_ROPE]
        rn = lax.rsqrt(jnp.sum(nope * nope, axis=0, keepdims=True) * (1.0 / QK_NOPE) + EPS)
        rr = lax.rsqrt((jnp.sum(x1 * x1, axis=0, keepdims=True)
                        + jnp.sum(x2 * x2, axis=0, keepdims=True)) * (1.0 / QK_ROPE) + EPS)
        y1 = x1 * rr * g_r1
        y2 = x2 * rr * g_r2
        qh = jnp.concatenate([nope * rn * g_nope, y1 * cos - y2 * sin,
                              y1 * sin + y2 * cos, zpad], axis=0)
        qt_ref[h * HEAD_PAD:(h + 1) * HEAD_PAD, :] = (qh * Q_SCALE).astype(BF16)

    vn = _rms(v_pre, g_v[...])
    if emit_vn:
        vn_ref[...] = vn
    vb = vn.astype(BF16)
    nc = tm // GM_CHUNK
    tril = (lax.broadcasted_iota(jnp.int32, (GM_CHUNK, GM_CHUNK), 0)
            >= lax.broadcasted_iota(jnp.int32, (GM_CHUNK, GM_CHUNK), 1))
    for g in range(GM_GROUPS):
        cs = slice(g * LANE, (g + 1) * LANE)
        w = jnp.where(tril, ws_ref[g], 0.0).astype(BF16)
        rhs = jnp.concatenate(
            [vb[c_ * GM_CHUNK:(c_ + 1) * GM_CHUNK, cs] for c_ in range(nc)], axis=1)
        mixed = _dot(w, rhs)
        for c_ in range(nc):
            rs = slice(c_ * GM_CHUNK, (c_ + 1) * GM_CHUNK)
            ob = u[rs, cs] * (mixed[:, c_ * LANE:(c_ + 1) * LANE] + bs_ref[g])
            gb_ref[rs, cs] = gbz[rs, cs] * ob


def _mix_in(layer, h, tab, tabt, wp, ws_eff, bs_eff, tm, emit_v, emit_vn):
    rows = h.shape[0]
    row = lambda w: pl.BlockSpec((tm, w), lambda i: (i, 0))
    col = lambda r: pl.BlockSpec((r, tm), lambda i: (0, i))
    head = lambda w: pl.BlockSpec((N_HEADS, tm, w), lambda i: (0, i, 0))
    g_qt = jnp.broadcast_to(wp["g_qt"][layer], (QK_NOPE + QK_ROPE, tm))
    consts = [g_qt, ws_eff, bs_eff]
    stacked = [wp[k] for k in ("g_mix", "w_q", "w_kv", "w_kr", "w_u", "w_v", "w_ga", "w_gb",
                               "g_qa", "w_uqt", "g_kva", "g_kr", "w_uk", "w_uvt", "g_kn",
                               "g_v")]
    out_shape = [jax.ShapeDtypeStruct((N_HEADS * HEAD_PAD, rows), BF16),
                 jax.ShapeDtypeStruct((N_HEADS, rows, HEAD_PAD), BF16),
                 jax.ShapeDtypeStruct((rows, KV_LORA), F32),
                 jax.ShapeDtypeStruct((QK_ROPE, rows), F32),
                 jax.ShapeDtypeStruct((rows, D_MODEL), F32),
                 jax.ShapeDtypeStruct((rows, D_MODEL), F32)]
    out_specs = [col(N_HEADS * HEAD_PAD), head(HEAD_PAD), row(KV_LORA), col(QK_ROPE),
                 row(D_MODEL), row(D_MODEL)]
    if emit_v:
        per = ATT_TK // tm
        out_shape.append(jax.ShapeDtypeStruct((N_HEADS, rows // ATT_TK, V_DIM, ATT_TK), BF16))
        out_specs.append(pl.BlockSpec((N_HEADS, 1, V_DIM, tm),
                                      lambda i: (0, i // per, 0, i % per)))
    if emit_vn:
        out_shape.append(jax.ShapeDtypeStruct((rows, GM_WIDTH), F32))
        out_specs.append(row(GM_WIDTH))
    return pl.pallas_call(
        functools.partial(_mix_kernel, tm=tm, emit_v=emit_v, emit_vn=emit_vn),
        grid=(rows // tm,),
        in_specs=[row(D_MODEL), row(2 * LANE), col(QK_ROPE)]
        + [_const_spec(a.shape) for a in consts]
        + [_layer_spec(a, layer) for a in stacked],
        out_specs=out_specs,
        out_shape=out_shape,
        compiler_params=_params("parallel"),
        name="mix_in",
    )(h, tab, tabt, *consts, *stacked)


def _attn_kernel(qt_ref, k_ref, vt_ref, o_ref):
    tq, tk, nh = ATT_TQ, ATT_TK, ATT_HEADS
    i = pl.program_id(1)
    ns, ws = tq // ATT_SUB, ATT_SUB
    chains = [(h, a) for h in range(nh) for a in range(ns)]
    qts = {(h, a): qt_ref[h * HEAD_PAD:(h + 1) * HEAD_PAD, a * ws:(a + 1) * ws]
           for h, a in chains}

    def scores(j, ch):
        start = pl.multiple_of(j * tk, tk)
        return _dot(k_ref[ch[0], pl.ds(start, tk), :], qts[ch])

    n_full = (i * tq) // tk
    dk = (lax.broadcasted_iota(jnp.int32, (tk, ws), 0) // CHUNK
          - lax.broadcasted_iota(jnp.int32, (tk, ws), 1) // CHUNK)
    off = (i * tq - n_full * tk) // CHUNK
    masks = [dk <= off + a * (ws // CHUNK) for a in range(ns)]

    def write(ch, l, acc):
        h, a = ch
        o_ref[a * ws:(a + 1) * ws, h * V_DIM:(h + 1) * V_DIM] = (acc / l).T

    def update(st, m, l, jump):
        p = jnp.exp2(st - m)
        m_new = jnp.maximum(m, jnp.max(st, axis=0, keepdims=True))
        alpha = jnp.exp2(m - m_new)
        l = alpha * (l + jnp.sum(p, axis=0, keepdims=True))
        return m_new, l, jnp.maximum(jump, m_new - m), alpha, p.astype(BF16)

    def body1(j, states):
        sts = [scores(j, ch) for ch in chains]
        ups = [update(st, m, l, jump) for st, (m, l, _, jump) in zip(sts, states)]
        new_states = []
        for ch, state, (m, l, jump, alpha, p) in zip(chains, states, ups):
            acc = alpha * (state[2] + _dot(vt_ref[ch[0], j], p))
            new_states.append((m, l, acc, jump))
        return tuple(new_states)

    def body_n(jj, states):
        js = [ATT_UNROLL * jj + t for t in range(ATT_UNROLL)]
        sts = [[scores(j, ch) for ch in chains] for j in js]
        ups, prev = [], [(m, l, jump) for (m, l, _, jump) in states]
        for t in range(ATT_UNROLL):
            up = [update(st, *pv) for st, pv in zip(sts[t], prev)]
            ups.append(up)
            prev = [(u[0], u[1], u[2]) for u in up]
        pvs = [[_dot(vt_ref[ch[0], js[t]], up[4]) for ch, up in zip(chains, ups[t])]
               for t in range(ATT_UNROLL)]
        new_states = []
        for c_, state in enumerate(states):
            acc = state[2]
            for t in range(ATT_UNROLL):
                acc = ups[t][c_][3] * (acc + pvs[t][c_])
            last = ups[-1][c_]
            new_states.append((last[0], last[1], acc, last[2]))
        return tuple(new_states)

    init = tuple(
        (jnp.max(_dot(k_ref[ch[0], 0:CHUNK, :], qts[ch]), axis=0, keepdims=True),
         jnp.zeros((1, ws), F32), jnp.zeros((V_DIM, ws), F32), jnp.zeros((1, ws), F32))
        for ch in chains)
    n_trips = n_full // ATT_UNROLL
    states = lax.fori_loop(0, n_trips, body_n, init)
    states = lax.fori_loop(n_trips * ATT_UNROLL, n_full, body1, states)
    worst = jnp.zeros((1, ws), F32)
    sts = [jnp.where(masks[ch[1]], scores(n_full, ch), NEG) for ch in chains]
    ups = [update(st, m, l, jump) for st, (m, l, _, jump) in zip(sts, states)]
    for ch, state, (_, l, jump, alpha, p) in zip(chains, states, ups):
        write(ch, l, alpha * (state[2] + _dot(vt_ref[ch[0], n_full], p)))
        worst = jnp.maximum(worst, jump)

    @pl.when(jnp.max(worst) > ATT_MAX_JUMP)
    def _():
        def exact(st, state):
            m, l, acc = state
            m_new = jnp.maximum(m, jnp.max(st, axis=0, keepdims=True))
            alpha = jnp.exp2(m - m_new)
            p = jnp.exp2(st - m_new)
            return m_new, alpha * l + jnp.sum(p, axis=0, keepdims=True), alpha * acc, p

        for ch in chains:
            def ebody(j, state, ch=ch):
                m, l, acc, p = exact(scores(j, ch), state)
                return m, l, acc + _dot(vt_ref[ch[0], j], p.astype(BF16))

            state = lax.fori_loop(0, n_full, ebody, (jnp.full((1, ws), NEG, F32),
                                                    jnp.zeros((1, ws), F32),
                                                    jnp.zeros((V_DIM, ws), F32)))
            st = jnp.where(masks[ch[1]], scores(n_full, ch), NEG)
            _, l, acc, p = exact(st, state)
            write(ch, l, acc + _dot(vt_ref[ch[0], n_full], p.astype(BF16)))


def _prompt_attention(qt, k, vt):
    s = k.shape[1]
    nh = ATT_HEADS
    once = pl.Buffered(1)
    return pl.pallas_call(
        _attn_kernel,
        grid=(N_HEADS // nh, s // ATT_TQ),
        in_specs=[pl.BlockSpec((nh * HEAD_PAD, ATT_TQ), lambda h, i: (h, i)),
                  pl.BlockSpec((nh, s, HEAD_PAD), lambda h, i: (h, 0, 0), pipeline_mode=once),
                  pl.BlockSpec((nh, s // ATT_TK, V_DIM, ATT_TK), lambda h, i: (h, 0, 0, 0),
                               pipeline_mode=once)],
        out_specs=pl.BlockSpec((ATT_TQ, nh * V_DIM), lambda h, i: (i, h)),
        out_shape=jax.ShapeDtypeStruct((s, N_HEADS * V_DIM), F32),
        compiler_params=_params("parallel", "arbitrary"),
        name="prompt_attn",
    )(qt, k, vt)


def _sample_attn_kernel(q_ref, kn_ref, cn_ref, cc_ref, ckrt_ref, w_ukt, w_uv, g_kn,
                        o_ref, s_scr, cb_scr, *, dec, past):
    tkc = SAMPLE_TK
    nch = past // tkc
    qg = [(q_ref[h][:, :QK_NOPE].astype(F32) * g_kn[...]).astype(BF16) for h in range(N_HEADS)]
    qr_all = jnp.concatenate(
        [q_ref[h][:, QK_NOPE:QK_NOPE + QK_ROPE] for h in range(N_HEADS)], axis=0)

    for j in range(nch):
        cb = cc_ref[0, 0, j * tkc:(j + 1) * tkc, :].astype(BF16)
        cb_scr[j] = cb
        knt = _dot_nt(w_ukt[...], cb)
        s_rope = _dot(qr_all, ckrt_ref[0, 0, :, j * tkc:(j + 1) * tkc].astype(BF16))
        for h in range(N_HEADS):
            kh = knt[h * QK_NOPE:(h + 1) * QK_NOPE]
            r = lax.rsqrt(jnp.sum(kh * kh, axis=0, keepdims=True) * (1.0 / QK_NOPE) + EPS)
            s_scr[j, h * dec:(h + 1) * dec, :] = (
                _dot(qg[h], kh.astype(BF16)) * r + s_rope[h * dec:(h + 1) * dec])

    s_new = jnp.concatenate([_dot_nt(q_ref[h], kn_ref[h]) for h in range(N_HEADS)], axis=0)
    m = jnp.max(s_new, axis=-1, keepdims=True)
    for j in range(nch):
        m = jnp.maximum(m, jnp.max(s_scr[j], axis=-1, keepdims=True))
    p_n = jnp.exp2(s_new - m)
    l = jnp.sum(p_n, axis=-1, keepdims=True)
    pc = _dot(p_n.astype(BF16), cn_ref[...].astype(BF16))
    for j in range(nch):
        p_c = jnp.exp2(s_scr[j] - m)
        l = l + jnp.sum(p_c, axis=-1, keepdims=True)
        pc = pc + _dot(p_c.astype(BF16), cb_scr[j])
    pcb = (pc / l).astype(BF16)
    for h in range(N_HEADS):
        o_ref[:, h * V_DIM:(h + 1) * V_DIM] = _dot(
            pcb[h * dec:(h + 1) * dec], w_uv[:, h * V_DIM:(h + 1) * V_DIM])


def _sample_attention(layer, q, k_new, c_new, cache_c, cache_krt, wp, dec):
    _, nb, past, _ = cache_c.shape
    rows = nb * dec
    consts = [wp["w_ukt"], wp["w_uv"], wp["g_kn"]]
    return pl.pallas_call(
        functools.partial(_sample_attn_kernel, dec=dec, past=past),
        grid=(nb,),
        in_specs=[pl.BlockSpec((N_HEADS, dec, HEAD_PAD), lambda b: (0, b, 0)),
                  pl.BlockSpec((N_HEADS, dec, HEAD_PAD), lambda b: (0, b, 0)),
                  pl.BlockSpec((dec, KV_LORA), lambda b: (b, 0)),
                  pl.BlockSpec((1, 1, past, KV_LORA), lambda b: (layer, b, 0, 0)),
                  pl.BlockSpec((1, 1, QK_ROPE, past), lambda b: (layer, b, 0, 0))]
        + [_layer_spec(a, layer) for a in consts],
        out_specs=pl.BlockSpec((dec, N_HEADS * V_DIM), lambda b: (b, 0)),
        out_shape=jax.ShapeDtypeStruct((rows, N_HEADS * V_DIM), F32),
        scratch_shapes=[pltpu.VMEM((past // SAMPLE_TK, N_HEADS * dec, SAMPLE_TK), F32),
                        pltpu.VMEM((past // SAMPLE_TK, SAMPLE_TK, KV_LORA), BF16)],
        compiler_params=_params("parallel"),
        name="sample_attn",
    )(q, k_new, c_new, cache_c, cache_krt, *consts)


def _post_kernel(h_ref, sa_ref, oa_ref, gb_ref, p_ref, w_o, g_f2, wg, wu, wd,
                 g_ple, w_pg, w_pe, o_ref):
    mixed = (sa_ref[...] * oa_ref[...] + gb_ref[...]).astype(BF16)
    h = h_ref[...] + _dot(mixed, w_o[...])
    h = _swiglu_half(h, g_f2[...], wg, wu, wd)
    gate = jax.nn.sigmoid(_dot(_rms(h, g_ple[...]).astype(BF16), w_pg[...]))
    o_ref[...] = h + gate * _dot(p_ref[0].astype(BF16), w_pe[...])


def _post(layer, h, sa, oa, gb, p, wp, tm):
    rows = h.shape[0]
    row = lambda w: pl.BlockSpec((tm, w), lambda i: (i, 0))
    consts = [wp[k] for k in ("w_o", "g_f2", "f2_wg", "f2_wu", "f2_wd", "g_ple", "w_pg", "w_pe")]
    return pl.pallas_call(
        _post_kernel,
        grid=(rows // tm,),
        in_specs=[row(D_MODEL)] * 4
        + [pl.BlockSpec((1, tm, PLE_DIM), lambda i: (layer, i, 0))]
        + [_layer_spec(a, layer) for a in consts],
        out_specs=row(D_MODEL),
        out_shape=jax.ShapeDtypeStruct(h.shape, F32),
        compiler_params=_params("parallel"),
        name="post",
    )(h, sa, oa, gb, p, *consts)


def _rope_tables(pos):
    inv = ROPE_THETA ** (-jnp.arange(HALF, dtype=F32) / HALF)
    posf = pos.astype(F32)
    zeros = jnp.zeros((LANE - QK_ROPE,), F32)
    ang = posf[:, None] * jnp.concatenate([inv, inv, zeros])[None, :]
    keep = jnp.concatenate([jnp.ones((QK_ROPE,), F32), zeros])
    sign = jnp.concatenate([-jnp.ones((HALF,), F32), jnp.ones((HALF,), F32), zeros])
    tab = jnp.concatenate([jnp.cos(ang) * keep, jnp.sin(ang) * sign], axis=1)
    ang_t = inv[:, None] * posf[None, :]
    tabt = jnp.concatenate([jnp.cos(ang_t), jnp.sin(ang_t)], axis=0)
    return tab, tabt


def _prep_weights(w):
    row = lambda a: a[:, None, :].astype(F32)
    pad_lane = lambda a: jnp.pad(a, [(0, 0)] * (a.ndim - 1) + [(0, LANE - a.shape[-1])])
    b = lambda a: a.astype(BF16)
    w_in = w["w_in"]
    o_kv = Q_LORA
    o_kr = o_kv + KV_LORA
    o_u = o_kr + QK_ROPE
    o_v = o_u + GM_WIDTH
    o_ga = o_v + GM_WIDTH
    o_gb = o_ga + D_MODEL
    w_uq = w["w_uq"].reshape(DEPTH, Q_LORA, N_HEADS, QK_NOPE + QK_ROPE)
    w_uq = jnp.pad(w_uq, ((0, 0), (0, 0), (0, 0), (0, HEAD_PAD - QK_NOPE - QK_ROPE)))
    w_uq = w_uq.reshape(DEPTH, Q_LORA, N_HEADS * HEAD_PAD)
    w_uv = w["w_uv"].reshape(DEPTH, KV_LORA, N_HEADS * V_DIM)
    w_uk = w["w_uk"].reshape(DEPTH, KV_LORA, N_HEADS * QK_NOPE)
    g_qt = jnp.concatenate([w["q_nope_norm"], w["q_rope_norm"]], axis=1)[:, :, None]
    return dict(
        g_f1=row(w["ffn1_norm"]), f1_wg=b(w["ffn1_w_gate"]), f1_wu=b(w["ffn1_w_up"]),
        f1_wd=b(w["ffn1_w_down"]),
        g_mix=row(w["mix_norm"]),
        w_q=b(w_in[:, :, :o_kv]), w_kv=b(w_in[:, :, o_kv:o_kr]),
        w_kr=b(pad_lane(w_in[:, :, o_kr:o_u])),
        w_u=b(w_in[:, :, o_u:o_v]), w_v=b(w_in[:, :, o_v:o_ga]),
        w_ga=b(w_in[:, :, o_ga:o_gb]), w_gb=b(w_in[:, :, o_gb:]),
        g_qa=row(w["q_a_norm"]), w_uqt=b(jnp.swapaxes(w_uq, 1, 2)), g_qt=g_qt.astype(F32),
        g_kva=row(w["kv_a_norm"]), g_kr=pad_lane(row(w["k_rope_norm"])),
        w_uk=b(w_uk), w_ukt=b(jnp.swapaxes(w_uk, 1, 2)),
        w_uv=b(w_uv), w_uvt=b(jnp.swapaxes(w_uv, 1, 2)),
        g_kn=row(w["k_nope_norm"]), g_v=row(w["gm_v_norm"]),
        w_o=b(w["w_o"]),
        g_f2=row(w["ffn2_norm"]), f2_wg=b(w["ffn2_w_gate"]), f2_wu=b(w["ffn2_w_up"]),
        f2_wd=b(w["ffn2_w_down"]),
        g_ple=row(w["ple_norm"]), w_pg=b(w["ple_w_gate"]), w_pe=b(w["ple_w_proj"]),
    )


def _gate_operands(w_s, b_s, length):
    reps = GM_CHUNK // length
    eye = jnp.eye(reps, dtype=w_s.dtype)
    ws = jnp.einsum("ab,gts->gatbs", eye, w_s[:, :length, :length])
    ws = ws.reshape(GM_GROUPS, GM_CHUNK, GM_CHUNK)
    bs = jnp.tile(b_s[:, :length], (1, reps))
    return ws, jnp.broadcast_to(bs[:, :, None], (GM_GROUPS, GM_CHUNK, LANE))


def kernel(x_prompt, x_sample, cache_kv_latent, cache_k_rope, p_prompt, p_sample, ffn1_norm, ffn1_w_gate, ffn1_w_up, ffn1_w_down, mix_norm, w_in, q_a_norm, w_uq, q_nope_norm, q_rope_norm, kv_a_norm, k_rope_norm, w_uk, w_uv, k_nope_norm, gm_v_norm, gm_w_s, gm_b_s, w_o, ffn2_norm, ffn2_w_gate, ffn2_w_up, ffn2_w_down, ple_norm, ple_w_gate, ple_w_proj):
    w = dict(ffn1_norm=ffn1_norm, ffn1_w_gate=ffn1_w_gate, ffn1_w_up=ffn1_w_up,
             ffn1_w_down=ffn1_w_down, mix_norm=mix_norm, w_in=w_in, q_a_norm=q_a_norm,
             w_uq=w_uq, q_nope_norm=q_nope_norm, q_rope_norm=q_rope_norm,
             kv_a_norm=kv_a_norm, k_rope_norm=k_rope_norm, w_uk=w_uk, w_uv=w_uv,
             k_nope_norm=k_nope_norm, gm_v_norm=gm_v_norm, w_o=w_o, ffn2_norm=ffn2_norm,
             ffn2_w_gate=ffn2_w_gate, ffn2_w_up=ffn2_w_up, ffn2_w_down=ffn2_w_down,
             ple_norm=ple_norm, ple_w_gate=ple_w_gate, ple_w_proj=ple_w_proj)
    batch, seq, _ = x_prompt.shape
    nb, dec, _ = x_sample.shape
    past = cache_kv_latent.shape[2]
    rows_s = nb * dec
    assert batch == 1 and seq % ATT_TK == 0 and rows_s % ROW_TILE == 0
    assert GM_CHUNK % dec == 0 and past % SAMPLE_TK == 0

    wp = _prep_weights(w)
    tab_p, tabt_p = _rope_tables(jnp.arange(seq, dtype=jnp.int32))
    tab_s, tabt_s = _rope_tables(past + jnp.arange(dec, dtype=jnp.int32))
    tab_s, tabt_s = jnp.tile(tab_s, (nb, 1)), jnp.tile(tabt_s, (1, nb))
    hp = x_prompt.reshape(seq, D_MODEL)
    hs = x_sample.reshape(rows_s, D_MODEL)
    pp = p_prompt.reshape(DEPTH, seq, PLE_DIM)
    ps = p_sample.reshape(DEPTH, rows_s, PLE_DIM)
    cache_krt = jnp.swapaxes(cache_k_rope, 2, 3)

    outs = {k: [] for k in ("pc", "pkr", "sc", "skr", "sv")}
    for l in range(DEPTH):
        ws_p, bs_p = _gate_operands(gm_w_s[l], gm_b_s[l], GM_CHUNK)
        ws_s, bs_s = _gate_operands(gm_w_s[l], gm_b_s[l], dec)

        hp = _ffn(l, hp, wp, FFN_TILE)
        qt, k, c, krt, sa, gb, vt = _mix_in(l, hp, tab_p, tabt_p, wp, ws_p, bs_p,
                                            ROW_TILE, True, False)
        oa = _prompt_attention(qt, k, vt)
        hp = _post(l, hp, sa, oa, gb, pp, wp, FFN_TILE)
        outs["pc"].append(c.reshape(batch, seq, KV_LORA))
        outs["pkr"].append(krt.reshape(batch, QK_ROPE, seq))

        hs = _ffn(l, hs, wp, ROW_TILE)
        qt, k, c, krt, sa, gb, vn = _mix_in(l, hs, tab_s, tabt_s, wp, ws_s, bs_s,
                                            ROW_TILE, False, True)
        q = qt.reshape(N_HEADS, HEAD_PAD, rows_s).transpose(0, 2, 1)
        oa = _sample_attention(l, q, k, c, cache_kv_latent, cache_krt, wp, dec)
        hs = _post(l, hs, sa, oa, gb, ps, wp, ROW_TILE)
        outs["sc"].append(c.reshape(nb, dec, KV_LORA))
        outs["skr"].append(krt.T.reshape(nb, dec, QK_ROPE))
        outs["sv"].append(vn.reshape(nb, dec, GM_WIDTH))

    return (hp.reshape(batch, seq, D_MODEL), hs.reshape(nb, dec, D_MODEL),
            jnp.stack(outs["pc"]), jnp.swapaxes(jnp.stack(outs["pkr"]), 2, 3),
            jnp.stack(outs["sc"]), jnp.stack(outs["skr"]), jnp.stack(outs["sv"]))
```

```python
import functools
import math

import jax
import jax.numpy as jnp
from jax import lax
from jax.experimental import pallas as pl
from jax.experimental.pallas import tpu as pltpu

D_MODEL = 1024
DEPTH = 2
CHUNK = 64
N_HEADS = 8
QK_NOPE = 128
QK_ROPE = 64
V_DIM = 128
Q_LORA = 384
KV_LORA = 512
GM_CHUNK = 128
GM_WIDTH = 1024
GM_GROUPS = 8
D_FF = 2816
PLE_DIM = 256
ROPE_THETA = 10000.0
EPS = 1e-6
SCALE = (QK_NOPE + QK_ROPE) ** -0.5
Q_SCALE = SCALE * math.log2(math.e)

LANE = 128
HEAD_PAD = 2 * LANE
HALF = QK_ROPE // 2
_C0 = Q_LORA + LANE
IN_COLS = dict(qkr=(0, _C0), kv=(_C0, _C0 + KV_LORA),
               u=(_C0 + KV_LORA, _C0 + KV_LORA + GM_WIDTH),
               v=(_C0 + KV_LORA + GM_WIDTH, _C0 + KV_LORA + 2 * GM_WIDTH),
               ga=(_C0 + KV_LORA + 2 * GM_WIDTH, _C0 + KV_LORA + 2 * GM_WIDTH + D_MODEL),
               gb=(_C0 + KV_LORA + 2 * GM_WIDTH + D_MODEL,
                   _C0 + KV_LORA + 2 * GM_WIDTH + 2 * D_MODEL))
VMEM_LIMIT = 56 * 1024 * 1024
FF_CHUNK = 256
ROW_TILE = 256
FFN_TILE = 512
ATT_TQ = 1024
ATT_SUB = 256
ATT_TK = 1024
ATT_HEADS = 2
ATT_UNROLL = 2
ATT_MAX_JUMP = 64.0
SAMPLE_TK = 512
NEG = -1e30

F32 = jnp.float32
BF16 = jnp.bfloat16


def _dot(a, b):
    return jnp.dot(a, b, preferred_element_type=F32)


def _dot_nt(a, b):
    return lax.dot_general(a, b, (((1,), (1,)), ((), ())), preferred_element_type=F32)


def _rms(x, g, n=None):
    n = x.shape[-1] if n is None else n
    ms = jnp.sum(x * x, axis=-1, keepdims=True) * (1.0 / n)
    return x * lax.rsqrt(ms + EPS) * g


def _const_spec(shape):
    nd = len(shape)
    return pl.BlockSpec(shape, lambda *_: (0,) * nd, pipeline_mode=pl.Buffered(1))


def _layer_spec(arr, layer):
    nd = arr.ndim - 1
    return pl.BlockSpec((None,) + arr.shape[1:], lambda *_: (layer,) + (0,) * nd,
                        pipeline_mode=pl.Buffered(1))


def _params(*sem):
    return pltpu.CompilerParams(dimension_semantics=sem, vmem_limit_bytes=VMEM_LIMIT)


def _swiglu_half(x, g, wg_ref, wu_ref, wd_ref):
    n = _rms(x, g).astype(BF16)
    acc = jnp.zeros(x.shape, F32)
    for c in range(D_FF // FF_CHUNK):
        sl = slice(c * FF_CHUNK, (c + 1) * FF_CHUNK)
        gt = _dot(n, wg_ref[:, sl])
        up = _dot(n, wu_ref[:, sl])
        a = (gt * jax.nn.sigmoid(gt) * up).astype(BF16)
        acc = acc + _dot(a, wd_ref[sl, :])
    return x + 0.5 * acc


def _ffn_kernel(x_ref, g_ref, wg_ref, wu_ref, wd_ref, o_ref):
    o_ref[...] = _swiglu_half(x_ref[...], g_ref[...], wg_ref, wu_ref, wd_ref)


def _ffn(layer, x, wp, tm):
    rows = x.shape[0]
    row = pl.BlockSpec((tm, D_MODEL), lambda i: (i, 0))
    consts = [wp["g_f1"], wp["f1_wg"], wp["f1_wu"], wp["f1_wd"]]
    return pl.pallas_call(
        _ffn_kernel,
        grid=(rows // tm,),
        in_specs=[row] + [_layer_spec(a, layer) for a in consts],
        out_specs=row,
        out_shape=jax.ShapeDtypeStruct(x.shape, F32),
        compiler_params=_params("parallel"),
        name="ffn",
    )(x, *consts)


def _rope_rows(x, tab):
    lane = lax.broadcasted_iota(jnp.int32, x.shape, 1)
    swapped = jnp.where(lane < HALF, pltpu.roll(x, LANE - HALF, 1), pltpu.roll(x, HALF, 1))
    return x * tab[:, :LANE] + swapped * tab[:, LANE:]


def _mix_kernel(h_ref, tab_ref, tabt_ref, g_qt, ws_ref, bs_ref,
                g_mix, w_all, g_qa, w_uqt, g_kva, g_kr, w_uk, w_uvt, g_kn, g_v,
                *out_refs, tm, emit_v, emit_vn):
    out_refs = list(out_refs)
    qt_ref, k_ref, c_ref, krt_ref, sa_ref, gb_ref = out_refs[:6]
    rest = out_refs[6:]
    vt_ref = rest.pop(0) if emit_v else None
    vn_ref = rest.pop(0) if emit_vn else None

    n = _rms(h_ref[...], g_mix[...]).astype(BF16)

    w_cols = lambda name: w_all[:, IN_COLS[name][0]:IN_COLS[name][1]]
    kv_pre = _dot(n, w_cols("kv"))
    qkr = _dot(n, w_cols("qkr"))
    q_pre, kr_pre = qkr[:, :Q_LORA], qkr[:, Q_LORA:]
    u = _dot(n, w_cols("u"))

    c = _rms(kv_pre, g_kva[...])
    c_ref[...] = c
    cb = c.astype(BF16)
    kn = _dot(cb, w_uk[...])
    gbz = jax.nn.sigmoid(_dot(n, w_cols("gb")))

    kr = _rope_rows(_rms(kr_pre, g_kr[...], QK_ROPE), tab_ref[...])
    krt_ref[...] = kr.T[:QK_ROPE]
    for h in range(N_HEADS):
        kh = _rms(kn[:, h * QK_NOPE:(h + 1) * QK_NOPE], g_kn[...])
        k_ref[h] = jnp.concatenate([kh, kr], axis=1).astype(BF16)

    ql = _rms(q_pre, g_qa[...]).astype(BF16)
    qft = _dot_nt(w_uqt[...], ql)
    if emit_v:
        vt = _dot_nt(w_uvt[...], cb)
        for h in range(N_HEADS):
            vt_ref[h, 0] = vt[h * V_DIM:(h + 1) * V_DIM].astype(BF16)
    v_pre = _dot(n, w_cols("v"))
    sa_ref[...] = jax.nn.sigmoid(_dot(n, w_cols("ga")))
    cos, sin = tabt_ref[:HALF, :], tabt_ref[HALF:, :]
    g_nope = g_qt[:QK_NOPE, :]
    g_r1 = g_qt[QK_NOPE:QK_NOPE + HALF, :]
    g_r2 = g_qt[QK_NOPE + HALF:, :]
    zpad = jnp.zeros((HEAD_PAD - QK_NOPE - QK_ROPE, tm), F32)
    for h in range(N_HEADS):
        blk = qft[h * HEAD_PAD:(h + 1) * HEAD_PAD]
        nope = blk[:QK_NOPE]
        x1 = blk[QK_NOPE:QK_NOPE + HALF]
        x2 = blk[QK_NOPE + HALF:QK_NOPE + QK_ROPE]
        rn = lax.rsqrt(jnp.sum(nope * nope, axis=0, keepdims=True) * (1.0 / QK_NOPE) + EPS)
        rr = lax.rsqrt((jnp.sum(x1 * x1, axis=0, keepdims=True)
                        + jnp.sum(x2 * x2, axis=0, keepdims=True)) * (1.0 / QK_ROPE) + EPS)
        y1 = x1 * rr * g_r1
        y2 = x2 * rr * g_r2
        qh = jnp.concatenate([nope * rn * g_nope, y1 * cos - y2 * sin,
                              y1 * sin + y2 * cos, zpad], axis=0)
        qt_ref[h * HEAD_PAD:(h + 1) * HEAD_PAD, :] = (qh * Q_SCALE).astype(BF16)

    vn = _rms(v_pre, g_v[...])
    if emit_vn:
        vn_ref[...] = vn
    vb = vn.astype(BF16)
    nc = tm // GM_CHUNK
    tril = (lax.broadcasted_iota(jnp.int32, (GM_CHUNK, GM_CHUNK), 0)
            >= lax.broadcasted_iota(jnp.int32, (GM_CHUNK, GM_CHUNK), 1))
    for g in range(GM_GROUPS):
        cs = slice(g * LANE, (g + 1) * LANE)
        w = jnp.where(tril, ws_ref[g], 0.0).astype(BF16)
        rhs = jnp.concatenate(
            [vb[c_ * GM_CHUNK:(c_ + 1) * GM_CHUNK, cs] for c_ in range(nc)], axis=1)
        mixed = _dot(w, rhs)
        for c_ in range(nc):
            rs = slice(c_ * GM_CHUNK, (c_ + 1) * GM_CHUNK)
            ob = u[rs, cs] * (mixed[:, c_ * LANE:(c_ + 1) * LANE] + bs_ref[g])
            gb_ref[rs, cs] = gbz[rs, cs] * ob


def _mix_in(layer, h, tab, tabt, wp, ws_eff, bs_eff, tm, emit_v, emit_vn):
    rows = h.shape[0]
    row = lambda w: pl.BlockSpec((tm, w), lambda i: (i, 0))
    col = lambda r: pl.BlockSpec((r, tm), lambda i: (0, i))
    head = lambda w: pl.BlockSpec((N_HEADS, tm, w), lambda i: (0, i, 0))
    g_qt = jnp.broadcast_to(wp["g_qt"][layer], (QK_NOPE + QK_ROPE, tm))
    consts = [g_qt, ws_eff, bs_eff]
    stacked = [wp[k] for k in ("g_mix", "w_all", "g_qa", "w_uqt", "g_kva", "g_kr", "w_uk",
                               "w_uvt", "g_kn", "g_v")]
    out_shape = [jax.ShapeDtypeStruct((N_HEADS * HEAD_PAD, rows), BF16),
                 jax.ShapeDtypeStruct((N_HEADS, rows, HEAD_PAD), BF16),
                 jax.ShapeDtypeStruct((rows, KV_LORA), F32),
                 jax.ShapeDtypeStruct((QK_ROPE, rows), F32),
                 jax.ShapeDtypeStruct((rows, D_MODEL), F32),
                 jax.ShapeDtypeStruct((rows, D_MODEL), F32)]
    out_specs = [col(N_HEADS * HEAD_PAD), head(HEAD_PAD), row(KV_LORA), col(QK_ROPE),
                 row(D_MODEL), row(D_MODEL)]
    if emit_v:
        per = ATT_TK // tm
        out_shape.append(jax.ShapeDtypeStruct((N_HEADS, rows // ATT_TK, V_DIM, ATT_TK), BF16))
        out_specs.append(pl.BlockSpec((N_HEADS, 1, V_DIM, tm),
                                      lambda i: (0, i // per, 0, i % per)))
    if emit_vn:
        out_shape.append(jax.ShapeDtypeStruct((rows, GM_WIDTH), F32))
        out_specs.append(row(GM_WIDTH))
    return pl.pallas_call(
        functools.partial(_mix_kernel, tm=tm, emit_v=emit_v, emit_vn=emit_vn),
        grid=(rows // tm,),
        in_specs=[row(D_MODEL), row(2 * LANE), col(QK_ROPE)]
        + [_const_spec(a.shape) for a in consts]
        + [_layer_spec(a, layer) for a in stacked],
        out_specs=out_specs,
        out_shape=out_shape,
        compiler_params=_params("parallel"),
        name="mix_in",
    )(h, tab, tabt, *consts, *stacked)


def _attn_kernel(qt_ref, k_ref, vt_ref, o_ref):
    tq, tk, nh = ATT_TQ, ATT_TK, ATT_HEADS
    i = pl.program_id(1)
    ns, ws = tq // ATT_SUB, ATT_SUB
    chains = [(h, a) for h in range(nh) for a in range(ns)]
    qts = {(h, a): qt_ref[h * HEAD_PAD:(h + 1) * HEAD_PAD, a * ws:(a + 1) * ws]
           for h, a in chains}

    def scores(j, ch):
        start = pl.multiple_of(j * tk, tk)
        return _dot(k_ref[ch[0], pl.ds(start, tk), :], qts[ch])

    n_full = (i * tq) // tk
    dk = (lax.broadcasted_iota(jnp.int32, (tk, ws), 0) // CHUNK
          - lax.broadcasted_iota(jnp.int32, (tk, ws), 1) // CHUNK)
    off = (i * tq - n_full * tk) // CHUNK
    masks = [dk <= off + a * (ws // CHUNK) for a in range(ns)]

    def write(ch, l, acc):
        h, a = ch
        o_ref[a * ws:(a + 1) * ws, h * V_DIM:(h + 1) * V_DIM] = (acc / l).T

    def update(st, m, l, jump):
        p = jnp.exp2(st - m)
        m_new = jnp.maximum(m, jnp.max(st, axis=0, keepdims=True))
        alpha = jnp.exp2(m - m_new)
        l = alpha * (l + jnp.sum(p, axis=0, keepdims=True))
        return m_new, l, jnp.maximum(jump, m_new - m), alpha, p.astype(BF16)

    def body1(j, states):
        sts = [scores(j, ch) for ch in chains]
        ups = [update(st, m, l, jump) for st, (m, l, _, jump) in zip(sts, states)]
        new_states = []
        for ch, state, (m, l, jump, alpha, p) in zip(chains, states, ups):
            acc = alpha * (state[2] + _dot(vt_ref[ch[0], j], p))
            new_states.append((m, l, acc, jump))
        return tuple(new_states)

    def body_n(jj, states):
        js = [ATT_UNROLL * jj + t for t in range(ATT_UNROLL)]
        sts = [[scores(j, ch) for ch in chains] for j in js]
        ups, prev = [], [(m, l, jump) for (m, l, _, jump) in states]
        for t in range(ATT_UNROLL):
            up = [update(st, *pv) for st, pv in zip(sts[t], prev)]
            ups.append(up)
            prev = [(u[0], u[1], u[2]) for u in up]
        pvs = [[_dot(vt_ref[ch[0], js[t]], up[4]) for ch, up in zip(chains, ups[t])]
               for t in range(ATT_UNROLL)]
        new_states = []
        for c_, state in enumerate(states):
            acc = state[2]
            for t in range(ATT_UNROLL):
                acc = ups[t][c_][3] * (acc + pvs[t][c_])
            last = ups[-1][c_]
            new_states.append((last[0], last[1], acc, last[2]))
        return tuple(new_states)

    init = tuple(
        (jnp.max(_dot(k_ref[ch[0], 0:CHUNK, :], qts[ch]), axis=0, keepdims=True),
         jnp.zeros((1, ws), F32), jnp.zeros((V_DIM, ws), F32), jnp.zeros((1, ws), F32))
        for ch in chains)
    n_trips = n_full // ATT_UNROLL
    states = lax.fori_loop(0, n_trips, body_n, init)
    states = lax.fori_loop(n_trips * ATT_UNROLL, n_full, body1, states)
    worst = jnp.zeros((1, ws), F32)
    sts = [jnp.where(masks[ch[1]], scores(n_full, ch), NEG) for ch in chains]
    ups = [update(st, m, l, jump) for st, (m, l, _, jump) in zip(sts, states)]
    for ch, state, (_, l, jump, alpha, p) in zip(chains, states, ups):
        write(ch, l, alpha * (state[2] + _dot(vt_ref[ch[0], n_full], p)))
        worst = jnp.maximum(worst, jump)

    @pl.when(jnp.max(worst) > ATT_MAX_JUMP)
    def _():
        def exact(st, state):
            m, l, acc = state
            m_new = jnp.maximum(m, jnp.max(st, axis=0, keepdims=True))
            alpha = jnp.exp2(m - m_new)
            p = jnp.exp2(st - m_new)
            return m_new, alpha * l + jnp.sum(p, axis=0, keepdims=True), alpha * acc, p

        for ch in chains:
            def ebody(j, state, ch=ch):
                m, l, acc, p = exact(scores(j, ch), state)
                return m, l, acc + _dot(vt_ref[ch[0], j], p.astype(BF16))

            state = lax.fori_loop(0, n_full, ebody, (jnp.full((1, ws), NEG, F32),
                                                    jnp.zeros((1, ws), F32),
                                                    jnp.zeros((V_DIM, ws), F32)))
            st = jnp.where(masks[ch[1]], scores(n_full, ch), NEG)
            _, l, acc, p = exact(st, state)
            write(ch, l, acc + _dot(vt_ref[ch[0], n_full], p.astype(BF16)))


def _prompt_attention(qt, k, vt):
    s = k.shape[1]
    nh = ATT_HEADS
    once = pl.Buffered(1)
    return pl.pallas_call(
        _attn_kernel,
        grid=(N_HEADS // nh, s // ATT_TQ),
        in_specs=[pl.BlockSpec((nh * HEAD_PAD, ATT_TQ), lambda h, i: (h, i)),
                  pl.BlockSpec((nh, s, HEAD_PAD), lambda h, i: (h, 0, 0), pipeline_mode=once),
                  pl.BlockSpec((nh, s // ATT_TK, V_DIM, ATT_TK), lambda h, i: (h, 0, 0, 0),
                               pipeline_mode=once)],
        out_specs=pl.BlockSpec((ATT_TQ, nh * V_DIM), lambda h, i: (i, h)),
        out_shape=jax.ShapeDtypeStruct((s, N_HEADS * V_DIM), F32),
        compiler_params=_params("parallel", "arbitrary"),
        name="prompt_attn",
    )(qt, k, vt)


def _sample_attn_kernel(q_ref, kn_ref, cn_ref, cc_ref, ckrt_ref, w_ukt, w_uv, g_kn,
                        o_ref, s_scr, cb_scr, *, dec, past):
    tkc = SAMPLE_TK
    nch = past // tkc
    qg = [(q_ref[h][:, :QK_NOPE].astype(F32) * g_kn[...]).astype(BF16) for h in range(N_HEADS)]
    qr_all = jnp.concatenate(
        [q_ref[h][:, QK_NOPE:QK_NOPE + QK_ROPE] for h in range(N_HEADS)], axis=0)

    for j in range(nch):
        cb = cc_ref[0, 0, j * tkc:(j + 1) * tkc, :].astype(BF16)
        cb_scr[j] = cb
        knt = _dot_nt(w_ukt[...], cb)
        s_rope = _dot(qr_all, ckrt_ref[0, 0, :, j * tkc:(j + 1) * tkc].astype(BF16))
        for h in range(N_HEADS):
            kh = knt[h * QK_NOPE:(h + 1) * QK_NOPE]
            r = lax.rsqrt(jnp.sum(kh * kh, axis=0, keepdims=True) * (1.0 / QK_NOPE) + EPS)
            s_scr[j, h * dec:(h + 1) * dec, :] = (
                _dot(qg[h], kh.astype(BF16)) * r + s_rope[h * dec:(h + 1) * dec])

    s_new = jnp.concatenate([_dot_nt(q_ref[h], kn_ref[h]) for h in range(N_HEADS)], axis=0)
    m = jnp.max(s_new, axis=-1, keepdims=True)
    for j in range(nch):
        m = jnp.maximum(m, jnp.max(s_scr[j], axis=-1, keepdims=True))
    p_n = jnp.exp2(s_new - m)
    l = jnp.sum(p_n, axis=-1, keepdims=True)
    pc = _dot(p_n.astype(BF16), cn_ref[...].astype(BF16))
    for j in range(nch):
        p_c = jnp.exp2(s_scr[j] - m)
        l = l + jnp.sum(p_c, axis=-1, keepdims=True)
        pc = pc + _dot(p_c.astype(BF16), cb_scr[j])
    pcb = (pc / l).astype(BF16)
    for h in range(N_HEADS):
        o_ref[:, h * V_DIM:(h + 1) * V_DIM] = _dot(
            pcb[h * dec:(h + 1) * dec], w_uv[:, h * V_DIM:(h + 1) * V_DIM])


def _sample_attention(layer, q, k_new, c_new, cache_c, cache_krt, wp, dec):
    _, nb, past, _ = cache_c.shape
    rows = nb * dec
    consts = [wp["w_ukt"], wp["w_uv"], wp["g_kn"]]
    return pl.pallas_call(
        functools.partial(_sample_attn_kernel, dec=dec, past=past),
        grid=(nb,),
        in_specs=[pl.BlockSpec((N_HEADS, dec, HEAD_PAD), lambda b: (0, b, 0)),
                  pl.BlockSpec((N_HEADS, dec, HEAD_PAD), lambda b: (0, b, 0)),
                  pl.BlockSpec((dec, KV_LORA), lambda b: (b, 0)),
                  pl.BlockSpec((1, 1, past, KV_LORA), lambda b: (layer, b, 0, 0)),
                  pl.BlockSpec((1, 1, QK_ROPE, past), lambda b: (layer, b, 0, 0))]
        + [_layer_spec(a, layer) for a in consts],
        out_specs=pl.BlockSpec((dec, N_HEADS * V_DIM), lambda b: (b, 0)),
        out_shape=jax.ShapeDtypeStruct((rows, N_HEADS * V_DIM), F32),
        scratch_shapes=[pltpu.VMEM((past // SAMPLE_TK, N_HEADS * dec, SAMPLE_TK), F32),
                        pltpu.VMEM((past // SAMPLE_TK, SAMPLE_TK, KV_LORA), BF16)],
        compiler_params=_params("parallel"),
        name="sample_attn",
    )(q, k_new, c_new, cache_c, cache_krt, *consts)


def _post_kernel(h_ref, sa_ref, oa_ref, gb_ref, p_ref, w_o, g_f2, wg, wu, wd,
                 g_ple, w_pg, w_pe, o_ref):
    mixed = (sa_ref[...] * oa_ref[...] + gb_ref[...]).astype(BF16)
    h = h_ref[...] + _dot(mixed, w_o[...])
    h = _swiglu_half(h, g_f2[...], wg, wu, wd)
    gate = jax.nn.sigmoid(_dot(_rms(h, g_ple[...]).astype(BF16), w_pg[...]))
    o_ref[...] = h + gate * _dot(p_ref[0].astype(BF16), w_pe[...])


def _post(layer, h, sa, oa, gb, p, wp, tm):
    rows = h.shape[0]
    row = lambda w: pl.BlockSpec((tm, w), lambda i: (i, 0))
    consts = [wp[k] for k in ("w_o", "g_f2", "f2_wg", "f2_wu", "f2_wd", "g_ple", "w_pg", "w_pe")]
    return pl.pallas_call(
        _post_kernel,
        grid=(rows // tm,),
        in_specs=[row(D_MODEL)] * 4
        + [pl.BlockSpec((1, tm, PLE_DIM), lambda i: (layer, i, 0))]
        + [_layer_spec(a, layer) for a in consts],
        out_specs=row(D_MODEL),
        out_shape=jax.ShapeDtypeStruct(h.shape, F32),
        compiler_params=_params("parallel"),
        name="post",
    )(h, sa, oa, gb, p, *consts)


def _rope_tables(pos):
    inv = ROPE_THETA ** (-jnp.arange(HALF, dtype=F32) / HALF)
    posf = pos.astype(F32)
    zeros = jnp.zeros((LANE - QK_ROPE,), F32)
    ang = posf[:, None] * jnp.concatenate([inv, inv, zeros])[None, :]
    keep = jnp.concatenate([jnp.ones((QK_ROPE,), F32), zeros])
    sign = jnp.concatenate([-jnp.ones((HALF,), F32), jnp.ones((HALF,), F32), zeros])
    tab = jnp.concatenate([jnp.cos(ang) * keep, jnp.sin(ang) * sign], axis=1)
    ang_t = inv[:, None] * posf[None, :]
    tabt = jnp.concatenate([jnp.cos(ang_t), jnp.sin(ang_t)], axis=0)
    return tab, tabt


def _prep_weights(w):
    row = lambda a: a[:, None, :].astype(F32)
    pad_lane = lambda a: jnp.pad(a, [(0, 0)] * (a.ndim - 1) + [(0, LANE - a.shape[-1])])
    b = lambda a: a.astype(BF16)
    w_in = w["w_in"]
    o_kv = Q_LORA
    o_kr = o_kv + KV_LORA
    o_u = o_kr + QK_ROPE
    w_uq = w["w_uq"].reshape(DEPTH, Q_LORA, N_HEADS, QK_NOPE + QK_ROPE)
    w_uq = jnp.pad(w_uq, ((0, 0), (0, 0), (0, 0), (0, HEAD_PAD - QK_NOPE - QK_ROPE)))
    w_uq = w_uq.reshape(DEPTH, Q_LORA, N_HEADS * HEAD_PAD)
    w_uv = w["w_uv"].reshape(DEPTH, KV_LORA, N_HEADS * V_DIM)
    w_uk = w["w_uk"].reshape(DEPTH, KV_LORA, N_HEADS * QK_NOPE)
    g_qt = jnp.concatenate([w["q_nope_norm"], w["q_rope_norm"]], axis=1)[:, :, None]
    return dict(
        g_f1=row(w["ffn1_norm"]), f1_wg=b(w["ffn1_w_gate"]), f1_wu=b(w["ffn1_w_up"]),
        f1_wd=b(w["ffn1_w_down"]),
        g_mix=row(w["mix_norm"]),
        w_all=b(jnp.concatenate(
            [w_in[:, :, :o_kv], pad_lane(w_in[:, :, o_kr:o_u]), w_in[:, :, o_kv:o_kr],
             w_in[:, :, o_u:]], axis=-1)),
        g_qa=row(w["q_a_norm"]), w_uqt=b(jnp.swapaxes(w_uq, 1, 2)), g_qt=g_qt.astype(F32),
        g_kva=row(w["kv_a_norm"]), g_kr=pad_lane(row(w["k_rope_norm"])),
        w_uk=b(w_uk), w_ukt=b(jnp.swapaxes(w_uk, 1, 2)),
        w_uv=b(w_uv), w_uvt=b(jnp.swapaxes(w_uv, 1, 2)),
        g_kn=row(w["k_nope_norm"]), g_v=row(w["gm_v_norm"]),
        w_o=b(w["w_o"]),
        g_f2=row(w["ffn2_norm"]), f2_wg=b(w["ffn2_w_gate"]), f2_wu=b(w["ffn2_w_up"]),
        f2_wd=b(w["ffn2_w_down"]),
        g_ple=row(w["ple_norm"]), w_pg=b(w["ple_w_gate"]), w_pe=b(w["ple_w_proj"]),
    )


def _gate_operands(w_s, b_s, length):
    reps = GM_CHUNK // length
    eye = jnp.eye(reps, dtype=w_s.dtype)
    ws = jnp.einsum("ab,gts->gatbs", eye, w_s[:, :length, :length])
    ws = ws.reshape(GM_GROUPS, GM_CHUNK, GM_CHUNK)
    bs = jnp.tile(b_s[:, :length], (1, reps))
    return ws, jnp.broadcast_to(bs[:, :, None], (GM_GROUPS, GM_CHUNK, LANE))


def kernel(x_prompt, x_sample, cache_kv_latent, cache_k_rope, p_prompt, p_sample, ffn1_norm, ffn1_w_gate, ffn1_w_up, ffn1_w_down, mix_norm, w_in, q_a_norm, w_uq, q_nope_norm, q_rope_norm, kv_a_norm, k_rope_norm, w_uk, w_uv, k_nope_norm, gm_v_norm, gm_w_s, gm_b_s, w_o, ffn2_norm, ffn2_w_gate, ffn2_w_up, ffn2_w_down, ple_norm, ple_w_gate, ple_w_proj):
    w = dict(ffn1_norm=ffn1_norm, ffn1_w_gate=ffn1_w_gate, ffn1_w_up=ffn1_w_up,
             ffn1_w_down=ffn1_w_down, mix_norm=mix_norm, w_in=w_in, q_a_norm=q_a_norm,
             w_uq=w_uq, q_nope_norm=q_nope_norm, q_rope_norm=q_rope_norm,
             kv_a_norm=kv_a_norm, k_rope_norm=k_rope_norm, w_uk=w_uk, w_uv=w_uv,
             k_nope_norm=k_nope_norm, gm_v_norm=gm_v_norm, w_o=w_o, ffn2_norm=ffn2_norm,
             ffn2_w_gate=ffn2_w_gate, ffn2_w_up=ffn2_w_up, ffn2_w_down=ffn2_w_down,
             ple_norm=ple_norm, ple_w_gate=ple_w_gate, ple_w_proj=ple_w_proj)
    batch, seq, _ = x_prompt.shape
    nb, dec, _ = x_sample.shape
    past = cache_kv_latent.shape[2]
    rows_s = nb * dec
    assert batch == 1 and seq % ATT_TK == 0 and rows_s % ROW_TILE == 0
    assert GM_CHUNK % dec == 0 and past % SAMPLE_TK == 0

    wp = _prep_weights(w)
    tab_p, tabt_p = _rope_tables(jnp.arange(seq, dtype=jnp.int32))
    tab_s, tabt_s = _rope_tables(past + jnp.arange(dec, dtype=jnp.int32))
    tab_s, tabt_s = jnp.tile(tab_s, (nb, 1)), jnp.tile(tabt_s, (1, nb))
    hp = x_prompt.reshape(seq, D_MODEL)
    hs = x_sample.reshape(rows_s, D_MODEL)
    pp = p_prompt.reshape(DEPTH, seq, PLE_DIM)
    ps = p_sample.reshape(DEPTH, rows_s, PLE_DIM)
    cache_krt = jnp.swapaxes(cache_k_rope, 2, 3)

    outs = {k: [] for k in ("pc", "pkr", "sc", "skr", "sv")}
    for l in range(DEPTH):
        ws_p, bs_p = _gate_operands(gm_w_s[l], gm_b_s[l], GM_CHUNK)
        ws_s, bs_s = _gate_operands(gm_w_s[l], gm_b_s[l], dec)

        hp = _ffn(l, hp, wp, FFN_TILE)
        qt, k, c, krt, sa, gb, vt = _mix_in(l, hp, tab_p, tabt_p, wp, ws_p, bs_p,
                                            ROW_TILE, True, False)
        oa = _prompt_attention(qt, k, vt)
        hp = _post(l, hp, sa, oa, gb, pp, wp, FFN_TILE)
        outs["pc"].append(c.reshape(batch, seq, KV_LORA))
        outs["pkr"].append(krt.reshape(batch, QK_ROPE, seq))

        hs = _ffn(l, hs, wp, ROW_TILE)
        qt, k, c, krt, sa, gb, vn = _mix_in(l, hs, tab_s, tabt_s, wp, ws_s, bs_s,
                                            ROW_TILE, False, True)
        q = qt.reshape(N_HEADS, HEAD_PAD, rows_s).transpose(0, 2, 1)
        oa = _sample_attention(l, q, k, c, cache_kv_latent, cache_krt, wp, dec)
        hs = _post(l, hs, sa, oa, gb, ps, wp, ROW_TILE)
        outs["sc"].append(c.reshape(nb, dec, KV_LORA))
        outs["skr"].append(krt.T.reshape(nb, dec, QK_ROPE))
        outs["sv"].append(vn.reshape(nb, dec, GM_WIDTH))

    return (hp.reshape(batch, seq, D_MODEL), hs.reshape(nb, dec, D_MODEL),
            jnp.stack(outs["pc"]), jnp.swapaxes(jnp.stack(outs["pkr"]), 2, 3),
            jnp.stack(outs["sc"]), jnp.stack(outs["skr"]), jnp.stack(outs["sv"]))
```

```python
import functools
import math

import jax
import jax.numpy as jnp
from jax import lax
from jax.experimental import pallas as pl
from jax.experimental.pallas import tpu as pltpu

D_MODEL = 1024
DEPTH = 2
CHUNK = 64
N_HEADS = 8
QK_NOPE = 128
QK_ROPE = 64
V_DIM = 128
Q_LORA = 384
KV_LORA = 512
GM_CHUNK = 128
GM_WIDTH = 1024
GM_GROUPS = 8
D_FF = 2816
PLE_DIM = 256
ROPE_THETA = 10000.0
EPS = 1e-6
SCALE = (QK_NOPE + QK_ROPE) ** -0.5
Q_SCALE = SCALE * math.log2(math.e)

LANE = 128
HEAD_PAD = 2 * LANE
HALF = QK_ROPE // 2
_C0 = Q_LORA + LANE
IN_COLS = dict(qkr=(0, _C0), kv=(_C0, _C0 + KV_LORA),
               u=(_C0 + KV_LORA, _C0 + KV_LORA + GM_WIDTH),
               v=(_C0 + KV_LORA + GM_WIDTH, _C0 + KV_LORA + 2 * GM_WIDTH),
               ga=(_C0 + KV_LORA + 2 * GM_WIDTH, _C0 + KV_LORA + 2 * GM_WIDTH + D_MODEL),
               gb=(_C0 + KV_LORA + 2 * GM_WIDTH + D_MODEL,
                   _C0 + KV_LORA + 2 * GM_WIDTH + 2 * D_MODEL))
VMEM_LIMIT = 56 * 1024 * 1024
FF_CHUNK = 256
ROW_TILE = 256
FFN_TILE = 512
ATT_TQ = 1024
ATT_SUB = 256
ATT_TK = 1024
ATT_HEADS = 2
ATT_UNROLL = 2
ATT_MAX_JUMP = 64.0
SAMPLE_TK = 512
NEG = -1e30

F32 = jnp.float32
BF16 = jnp.bfloat16


def _dot(a, b):
    return jnp.dot(a, b, preferred_element_type=F32)


def _dot_nt(a, b):
    return lax.dot_general(a, b, (((1,), (1,)), ((), ())), preferred_element_type=F32)


def _rms(x, g, n=None):
    n = x.shape[-1] if n is None else n
    ms = jnp.sum(x * x, axis=-1, keepdims=True) * (1.0 / n)
    return x * lax.rsqrt(ms + EPS) * g


def _const_spec(shape):
    nd = len(shape)
    return pl.BlockSpec(shape, lambda *_: (0,) * nd, pipeline_mode=pl.Buffered(1))


def _layer_spec(arr, layer):
    nd = arr.ndim - 1
    return pl.BlockSpec((None,) + arr.shape[1:], lambda *_: (layer,) + (0,) * nd,
                        pipeline_mode=pl.Buffered(1))


def _params(*sem):
    return pltpu.CompilerParams(dimension_semantics=sem, vmem_limit_bytes=VMEM_LIMIT)


def _swiglu_half(x, g, wg_ref, wu_ref, wd_ref):
    n = _rms(x, g).astype(BF16)
    acc = jnp.zeros(x.shape, F32)
    for c in range(D_FF // FF_CHUNK):
        sl = slice(c * FF_CHUNK, (c + 1) * FF_CHUNK)
        gt = _dot(n, wg_ref[:, sl])
        up = _dot(n, wu_ref[:, sl])
        a = (gt * jax.nn.sigmoid(gt) * up).astype(BF16)
        acc = acc + _dot(a, wd_ref[sl, :])
    return x + 0.5 * acc


def _ffn_kernel(x_ref, g_ref, wg_ref, wu_ref, wd_ref, o_ref):
    o_ref[...] = _swiglu_half(x_ref[...], g_ref[...], wg_ref, wu_ref, wd_ref)


def _ffn(layer, x, wp, tm):
    rows = x.shape[0]
    row = pl.BlockSpec((tm, D_MODEL), lambda i: (i, 0))
    consts = [wp["g_f1"], wp["f1_wg"], wp["f1_wu"], wp["f1_wd"]]
    return pl.pallas_call(
        _ffn_kernel,
        grid=(rows // tm,),
        in_specs=[row] + [_layer_spec(a, layer) for a in consts],
        out_specs=row,
        out_shape=jax.ShapeDtypeStruct(x.shape, F32),
        compiler_params=_params("parallel"),
        name="ffn",
    )(x, *consts)


def _rope_rows(x, tabt):
    t = tabt.T
    lane = lax.broadcasted_iota(jnp.int32, x.shape, 1)
    lo, mid = lane < HALF, lane < QK_ROPE
    to_hi, to_lo = (lambda a: pltpu.roll(a, HALF, 1)), (lambda a: pltpu.roll(a, LANE - HALF, 1))
    cos2 = jnp.where(lo, t, jnp.where(mid, to_hi(t), 0.0))
    sin2 = jnp.where(lo, -to_lo(t), jnp.where(mid, t, 0.0))
    swapped = jnp.where(lo, to_lo(x), to_hi(x))
    return x * cos2 + swapped * sin2


def _mix_kernel(h_ref, tabt_ref, g_qt, ws_ref, bs_ref,
                g_mix, w_all, g_qa, w_uqt, g_kva, g_kr, w_uk, w_uvt, g_kn, g_v,
                *out_refs, tm, emit_v, emit_vn):
    out_refs = list(out_refs)
    qt_ref, k_ref, c_ref, krt_ref, sa_ref, gb_ref = out_refs[:6]
    rest = out_refs[6:]
    vt_ref = rest.pop(0) if emit_v else None
    vn_ref = rest.pop(0) if emit_vn else None

    n = _rms(h_ref[...], g_mix[...]).astype(BF16)

    w_cols = lambda name: w_all[:, IN_COLS[name][0]:IN_COLS[name][1]]
    kv_pre = _dot(n, w_cols("kv"))
    qkr = _dot(n, w_cols("qkr"))
    q_pre, kr_pre = qkr[:, :Q_LORA], qkr[:, Q_LORA:]
    u = _dot(n, w_cols("u"))

    c = _rms(kv_pre, g_kva[...])
    c_ref[...] = c
    cb = c.astype(BF16)
    kn = _dot(cb, w_uk[...])
    gbz = jax.nn.sigmoid(_dot(n, w_cols("gb")))

    kr = _rope_rows(_rms(kr_pre, g_kr[...], QK_ROPE), tabt_ref[...])
    krt_ref[...] = kr.T[:QK_ROPE]
    for h in range(N_HEADS):
        kh = _rms(kn[:, h * QK_NOPE:(h + 1) * QK_NOPE], g_kn[...])
        k_ref[h] = jnp.concatenate([kh, kr], axis=1).astype(BF16)

    ql = _rms(q_pre, g_qa[...]).astype(BF16)
    qft = _dot_nt(w_uqt[...], ql)
    if emit_v:
        vt = _dot_nt(w_uvt[...], cb)
        for h in range(N_HEADS):
            vt_ref[h, 0] = vt[h * V_DIM:(h + 1) * V_DIM].astype(BF16)
    v_pre = _dot(n, w_cols("v"))
    sa_ref[...] = jax.nn.sigmoid(_dot(n, w_cols("ga")))
    cos, sin = tabt_ref[:HALF, :], tabt_ref[HALF:QK_ROPE, :]
    g_nope = g_qt[:QK_NOPE, :]
    g_r1 = g_qt[QK_NOPE:QK_NOPE + HALF, :]
    g_r2 = g_qt[QK_NOPE + HALF:, :]
    zpad = jnp.zeros((HEAD_PAD - QK_NOPE - QK_ROPE, tm), F32)
    for h in range(N_HEADS):
        blk = qft[h * HEAD_PAD:(h + 1) * HEAD_PAD]
        nope = blk[:QK_NOPE]
        x1 = blk[QK_NOPE:QK_NOPE + HALF]
        x2 = blk[QK_NOPE + HALF:QK_NOPE + QK_ROPE]
        rn = lax.rsqrt(jnp.sum(nope * nope, axis=0, keepdims=True) * (1.0 / QK_NOPE) + EPS)
        rr = lax.rsqrt((jnp.sum(x1 * x1, axis=0, keepdims=True)
                        + jnp.sum(x2 * x2, axis=0, keepdims=True)) * (1.0 / QK_ROPE) + EPS)
        y1 = x1 * rr * g_r1
        y2 = x2 * rr * g_r2
        qh = jnp.concatenate([nope * rn * g_nope, y1 * cos - y2 * sin,
                              y1 * sin + y2 * cos, zpad], axis=0)
        qt_ref[h * HEAD_PAD:(h + 1) * HEAD_PAD, :] = (qh * Q_SCALE).astype(BF16)

    vn = _rms(v_pre, g_v[...])
    if emit_vn:
        vn_ref[...] = vn
    vb = vn.astype(BF16)
    nc = tm // GM_CHUNK
    tril = (lax.broadcasted_iota(jnp.int32, (GM_CHUNK, GM_CHUNK), 0)
            >= lax.broadcasted_iota(jnp.int32, (GM_CHUNK, GM_CHUNK), 1))
    for g in range(GM_GROUPS):
        cs = slice(g * LANE, (g + 1) * LANE)
        w = jnp.where(tril, ws_ref[g], 0.0).astype(BF16)
        rhs = jnp.concatenate(
            [vb[c_ * GM_CHUNK:(c_ + 1) * GM_CHUNK, cs] for c_ in range(nc)], axis=1)
        mixed = _dot(w, rhs)
        for c_ in range(nc):
            rs = slice(c_ * GM_CHUNK, (c_ + 1) * GM_CHUNK)
            ob = u[rs, cs] * (mixed[:, c_ * LANE:(c_ + 1) * LANE] + bs_ref[g])
            gb_ref[rs, cs] = gbz[rs, cs] * ob


def _mix_in(layer, h, tabt, wp, ws_eff, bs_eff, tm, emit_v, emit_vn):
    rows = h.shape[0]
    row = lambda w: pl.BlockSpec((tm, w), lambda i: (i, 0))
    col = lambda r: pl.BlockSpec((r, tm), lambda i: (0, i))
    head = lambda w: pl.BlockSpec((N_HEADS, tm, w), lambda i: (0, i, 0))
    g_qt = jnp.broadcast_to(wp["g_qt"][layer], (QK_NOPE + QK_ROPE, tm))
    consts = [g_qt, ws_eff, bs_eff]
    stacked = [wp[k] for k in ("g_mix", "w_all", "g_qa", "w_uqt", "g_kva", "g_kr", "w_uk",
                               "w_uvt", "g_kn", "g_v")]
    out_shape = [jax.ShapeDtypeStruct((N_HEADS * HEAD_PAD, rows), BF16),
                 jax.ShapeDtypeStruct((N_HEADS, rows, HEAD_PAD), BF16),
                 jax.ShapeDtypeStruct((rows, KV_LORA), F32),
                 jax.ShapeDtypeStruct((QK_ROPE, rows), F32),
                 jax.ShapeDtypeStruct((rows, D_MODEL), F32),
                 jax.ShapeDtypeStruct((rows, D_MODEL), F32)]
    out_specs = [col(N_HEADS * HEAD_PAD), head(HEAD_PAD), row(KV_LORA), col(QK_ROPE),
                 row(D_MODEL), row(D_MODEL)]
    if emit_v:
        per = ATT_TK // tm
        out_shape.append(jax.ShapeDtypeStruct((N_HEADS, rows // ATT_TK, V_DIM, ATT_TK), BF16))
        out_specs.append(pl.BlockSpec((N_HEADS, 1, V_DIM, tm),
                                      lambda i: (0, i // per, 0, i % per)))
    if emit_vn:
        out_shape.append(jax.ShapeDtypeStruct((rows, GM_WIDTH), F32))
        out_specs.append(row(GM_WIDTH))
    return pl.pallas_call(
        functools.partial(_mix_kernel, tm=tm, emit_v=emit_v, emit_vn=emit_vn),
        grid=(rows // tm,),
        in_specs=[row(D_MODEL), col(LANE)]
        + [_const_spec(a.shape) for a in consts]
        + [_layer_spec(a, layer) for a in stacked],
        out_specs=out_specs,
        out_shape=out_shape,
        compiler_params=_params("parallel"),
        name="mix_in",
    )(h, tabt, *consts, *stacked)


def _attn_kernel(qt_ref, k_ref, vt_ref, o_ref):
    tq, tk, nh = ATT_TQ, ATT_TK, ATT_HEADS
    i = pl.program_id(1)
    ns, ws = tq // ATT_SUB, ATT_SUB
    chains = [(h, a) for h in range(nh) for a in range(ns)]
    qts = {(h, a): qt_ref[h * HEAD_PAD:(h + 1) * HEAD_PAD, a * ws:(a + 1) * ws]
           for h, a in chains}

    def scores(j, ch):
        start = pl.multiple_of(j * tk, tk)
        return _dot(k_ref[ch[0], pl.ds(start, tk), :], qts[ch])

    n_full = (i * tq) // tk
    dk = (lax.broadcasted_iota(jnp.int32, (tk, ws), 0) // CHUNK
          - lax.broadcasted_iota(jnp.int32, (tk, ws), 1) // CHUNK)
    off = (i * tq - n_full * tk) // CHUNK
    masks = [dk <= off + a * (ws // CHUNK) for a in range(ns)]

    def write(ch, l, acc):
        h, a = ch
        o_ref[a * ws:(a + 1) * ws, h * V_DIM:(h + 1) * V_DIM] = (acc / l).T

    def update(st, m, l, jump):
        p = jnp.exp2(st - m)
        m_new = jnp.maximum(m, jnp.max(st, axis=0, keepdims=True))
        alpha = jnp.exp2(m - m_new)
        l = alpha * (l + jnp.sum(p, axis=0, keepdims=True))
        return m_new, l, jnp.maximum(jump, m_new - m), alpha, p.astype(BF16)

    def body1(j, states):
        sts = [scores(j, ch) for ch in chains]
        ups = [update(st, m, l, jump) for st, (m, l, _, jump) in zip(sts, states)]
        new_states = []
        for ch, state, (m, l, jump, alpha, p) in zip(chains, states, ups):
            acc = alpha * (state[2] + _dot(vt_ref[ch[0], j], p))
            new_states.append((m, l, acc, jump))
        return tuple(new_states)

    def body_n(jj, states):
        js = [ATT_UNROLL * jj + t for t in range(ATT_UNROLL)]
        sts = [[scores(j, ch) for ch in chains] for j in js]
        ups, prev = [], [(m, l, jump) for (m, l, _, jump) in states]
        for t in range(ATT_UNROLL):
            up = [update(st, *pv) for st, pv in zip(sts[t], prev)]
            ups.append(up)
            prev = [(u[0], u[1], u[2]) for u in up]
        pvs = [[_dot(vt_ref[ch[0], js[t]], up[4]) for ch, up in zip(chains, ups[t])]
               for t in range(ATT_UNROLL)]
        new_states = []
        for c_, state in enumerate(states):
            acc = state[2]
            for t in range(ATT_UNROLL):
                acc = ups[t][c_][3] * (acc + pvs[t][c_])
            last = ups[-1][c_]
            new_states.append((last[0], last[1], acc, last[2]))
        return tuple(new_states)

    init = tuple(
        (jnp.max(_dot(k_ref[ch[0], 0:CHUNK, :], qts[ch]), axis=0, keepdims=True),
         jnp.zeros((1, ws), F32), jnp.zeros((V_DIM, ws), F32), jnp.zeros((1, ws), F32))
        for ch in chains)
    n_trips = n_full // ATT_UNROLL
    states = lax.fori_loop(0, n_trips, body_n, init)
    states = lax.fori_loop(n_trips * ATT_UNROLL, n_full, body1, states)
    worst = jnp.zeros((1, ws), F32)
    sts = [jnp.where(masks[ch[1]], scores(n_full, ch), NEG) for ch in chains]
    ups = [update(st, m, l, jump) for st, (m, l, _, jump) in zip(sts, states)]
    for ch, state, (_, l, jump, alpha, p) in zip(chains, states, ups):
        write(ch, l, alpha * (state[2] + _dot(vt_ref[ch[0], n_full], p)))
        worst = jnp.maximum(worst, jump)

    @pl.when(jnp.max(worst) > ATT_MAX_JUMP)
    def _():
        def exact(st, state):
            m, l, acc = state
            m_new = jnp.maximum(m, jnp.max(st, axis=0, keepdims=True))
            alpha = jnp.exp2(m - m_new)
            p = jnp.exp2(st - m_new)
            return m_new, alpha * l + jnp.sum(p, axis=0, keepdims=True), alpha * acc, p

        for ch in chains:
            def ebody(j, state, ch=ch):
                m, l, acc, p = exact(scores(j, ch), state)
                return m, l, acc + _dot(vt_ref[ch[0], j], p.astype(BF16))

            state = lax.fori_loop(0, n_full, ebody, (jnp.full((1, ws), NEG, F32),
                                                    jnp.zeros((1, ws), F32),
                                                    jnp.zeros((V_DIM, ws), F32)))
            st = jnp.where(masks[ch[1]], scores(n_full, ch), NEG)
            _, l, acc, p = exact(st, state)
            write(ch, l, acc + _dot(vt_ref[ch[0], n_full], p.astype(BF16)))


def _prompt_attention(qt, k, vt):
    s = k.shape[1]
    nh = ATT_HEADS
    once = pl.Buffered(1)
    return pl.pallas_call(
        _attn_kernel,
        grid=(N_HEADS // nh, s // ATT_TQ),
        in_specs=[pl.BlockSpec((nh * HEAD_PAD, ATT_TQ), lambda h, i: (h, i)),
                  pl.BlockSpec((nh, s, HEAD_PAD), lambda h, i: (h, 0, 0), pipeline_mode=once),
                  pl.BlockSpec((nh, s // ATT_TK, V_DIM, ATT_TK), lambda h, i: (h, 0, 0, 0),
                               pipeline_mode=once)],
        out_specs=pl.BlockSpec((ATT_TQ, nh * V_DIM), lambda h, i: (i, h)),
        out_shape=jax.ShapeDtypeStruct((s, N_HEADS * V_DIM), F32),
        compiler_params=_params("parallel", "arbitrary"),
        name="prompt_attn",
    )(qt, k, vt)


def _sample_attn_kernel(q_ref, kn_ref, cn_ref, cc_ref, ckrt_ref, w_ukt, w_uv, g_kn,
                        o_ref, s_scr, cb_scr, *, dec, past):
    tkc = SAMPLE_TK
    nch = past // tkc
    qg = [(q_ref[h][:, :QK_NOPE].astype(F32) * g_kn[...]).astype(BF16) for h in range(N_HEADS)]
    qr_all = jnp.concatenate(
        [q_ref[h][:, QK_NOPE:QK_NOPE + QK_ROPE] for h in range(N_HEADS)], axis=0)

    for j in range(nch):
        cb = cc_ref[0, 0, j * tkc:(j + 1) * tkc, :].astype(BF16)
        cb_scr[j] = cb
        knt = _dot_nt(w_ukt[...], cb)
        s_rope = _dot(qr_all, ckrt_ref[0, 0, :, j * tkc:(j + 1) * tkc].astype(BF16))
        for h in range(N_HEADS):
            kh = knt[h * QK_NOPE:(h + 1) * QK_NOPE]
            r = lax.rsqrt(jnp.sum(kh * kh, axis=0, keepdims=True) * (1.0 / QK_NOPE) + EPS)
            s_scr[j, h * dec:(h + 1) * dec, :] = (
                _dot(qg[h], kh.astype(BF16)) * r + s_rope[h * dec:(h + 1) * dec])

    s_new = jnp.concatenate([_dot_nt(q_ref[h], kn_ref[h]) for h in range(N_HEADS)], axis=0)
    m = jnp.max(s_new, axis=-1, keepdims=True)
    for j in range(nch):
        m = jnp.maximum(m, jnp.max(s_scr[j], axis=-1, keepdims=True))
    p_n = jnp.exp2(s_new - m)
    l = jnp.sum(p_n, axis=-1, keepdims=True)
    pc = _dot(p_n.astype(BF16), cn_ref[...].astype(BF16))
    for j in range(nch):
        p_c = jnp.exp2(s_scr[j] - m)
        l = l + jnp.sum(p_c, axis=-1, keepdims=True)
        pc = pc + _dot(p_c.astype(BF16), cb_scr[j])
    pcb = (pc / l).astype(BF16)
    for h in range(N_HEADS):
        o_ref[:, h * V_DIM:(h + 1) * V_DIM] = _dot(
            pcb[h * dec:(h + 1) * dec], w_uv[:, h * V_DIM:(h + 1) * V_DIM])


def _sample_attention(layer, q, k_new, c_new, cache_c, cache_krt, wp, dec):
    _, nb, past, _ = cache_c.shape
    rows = nb * dec
    consts = [wp["w_ukt"], wp["w_uv"], wp["g_kn"]]
    return pl.pallas_call(
        functools.partial(_sample_attn_kernel, dec=dec, past=past),
        grid=(nb,),
        in_specs=[pl.BlockSpec((N_HEADS, dec, HEAD_PAD), lambda b: (0, b, 0)),
                  pl.BlockSpec((N_HEADS, dec, HEAD_PAD), lambda b: (0, b, 0)),
                  pl.BlockSpec((dec, KV_LORA), lambda b: (b, 0)),
                  pl.BlockSpec((1, 1, past, KV_LORA), lambda b: (layer, b, 0, 0)),
                  pl.BlockSpec((1, 1, QK_ROPE, past), lambda b: (layer, b, 0, 0))]
        + [_layer_spec(a, layer) for a in consts],
        out_specs=pl.BlockSpec((dec, N_HEADS * V_DIM), lambda b: (b, 0)),
        out_shape=jax.ShapeDtypeStruct((rows, N_HEADS * V_DIM), F32),
        scratch_shapes=[pltpu.VMEM((past // SAMPLE_TK, N_HEADS * dec, SAMPLE_TK), F32),
                        pltpu.VMEM((past // SAMPLE_TK, SAMPLE_TK, KV_LORA), BF16)],
        compiler_params=_params("parallel"),
        name="sample_attn",
    )(q, k_new, c_new, cache_c, cache_krt, *consts)


def _post_kernel(h_ref, sa_ref, oa_ref, gb_ref, p_ref, w_o, g_f2, wg, wu, wd,
                 g_ple, w_pg, w_pe, o_ref):
    mixed = (sa_ref[...] * oa_ref[...] + gb_ref[...]).astype(BF16)
    h = h_ref[...] + _dot(mixed, w_o[...])
    h = _swiglu_half(h, g_f2[...], wg, wu, wd)
    gate = jax.nn.sigmoid(_dot(_rms(h, g_ple[...]).astype(BF16), w_pg[...]))
    o_ref[...] = h + gate * _dot(p_ref[0].astype(BF16), w_pe[...])


def _post(layer, h, sa, oa, gb, p, wp, tm):
    rows = h.shape[0]
    row = lambda w: pl.BlockSpec((tm, w), lambda i: (i, 0))
    consts = [wp[k] for k in ("w_o", "g_f2", "f2_wg", "f2_wu", "f2_wd", "g_ple", "w_pg", "w_pe")]
    return pl.pallas_call(
        _post_kernel,
        grid=(rows // tm,),
        in_specs=[row(D_MODEL)] * 4
        + [pl.BlockSpec((1, tm, PLE_DIM), lambda i: (layer, i, 0))]
        + [_layer_spec(a, layer) for a in consts],
        out_specs=row(D_MODEL),
        out_shape=jax.ShapeDtypeStruct(h.shape, F32),
        compiler_params=_params("parallel"),
        name="post",
    )(h, sa, oa, gb, p, *consts)


def _rope_table(pos):
    inv = ROPE_THETA ** (-jnp.arange(HALF, dtype=F32) / HALF)
    ang_t = inv[:, None] * pos.astype(F32)[None, :]
    pad = jnp.zeros((LANE - QK_ROPE, pos.shape[0]), F32)
    return jnp.concatenate([jnp.cos(ang_t), jnp.sin(ang_t), pad], axis=0)


def _prep_weights(w):
    row = lambda a: a[:, None, :].astype(F32)
    pad_lane = lambda a: jnp.pad(a, [(0, 0)] * (a.ndim - 1) + [(0, LANE - a.shape[-1])])
    b = lambda a: a.astype(BF16)
    w_in = w["w_in"]
    o_kv = Q_LORA
    o_kr = o_kv + KV_LORA
    o_u = o_kr + QK_ROPE
    w_uq = w["w_uq"].reshape(DEPTH, Q_LORA, N_HEADS, QK_NOPE + QK_ROPE)
    w_uq = jnp.pad(w_uq, ((0, 0), (0, 0), (0, 0), (0, HEAD_PAD - QK_NOPE - QK_ROPE)))
    w_uq = w_uq.reshape(DEPTH, Q_LORA, N_HEADS * HEAD_PAD)
    w_uv = w["w_uv"].reshape(DEPTH, KV_LORA, N_HEADS * V_DIM)
    w_uk = w["w_uk"].reshape(DEPTH, KV_LORA, N_HEADS * QK_NOPE)
    g_qt = jnp.concatenate([w["q_nope_norm"], w["q_rope_norm"]], axis=1)[:, :, None]
    return dict(
        g_f1=row(w["ffn1_norm"]), f1_wg=b(w["ffn1_w_gate"]), f1_wu=b(w["ffn1_w_up"]),
        f1_wd=b(w["ffn1_w_down"]),
        g_mix=row(w["mix_norm"]),
        w_all=b(jnp.concatenate(
            [w_in[:, :, :o_kv], pad_lane(w_in[:, :, o_kr:o_u]), w_in[:, :, o_kv:o_kr],
             w_in[:, :, o_u:]], axis=-1)),
        g_qa=row(w["q_a_norm"]), w_uqt=b(jnp.swapaxes(w_uq, 1, 2)), g_qt=g_qt.astype(F32),
        g_kva=row(w["kv_a_norm"]), g_kr=pad_lane(row(w["k_rope_norm"])),
        w_uk=b(w_uk), w_ukt=b(jnp.swapaxes(w_uk, 1, 2)),
        w_uv=b(w_uv), w_uvt=b(jnp.swapaxes(w_uv, 1, 2)),
        g_kn=row(w["k_nope_norm"]), g_v=row(w["gm_v_norm"]),
        w_o=b(w["w_o"]),
        g_f2=row(w["ffn2_norm"]), f2_wg=b(w["ffn2_w_gate"]), f2_wu=b(w["ffn2_w_up"]),
        f2_wd=b(w["ffn2_w_down"]),
        g_ple=row(w["ple_norm"]), w_pg=b(w["ple_w_gate"]), w_pe=b(w["ple_w_proj"]),
    )


def _gate_operands(w_s, b_s, length):
    reps = GM_CHUNK // length
    eye = jnp.eye(reps, dtype=w_s.dtype)
    ws = jnp.einsum("ab,gts->gatbs", eye, w_s[:, :length, :length])
    ws = ws.reshape(GM_GROUPS, GM_CHUNK, GM_CHUNK)
    bs = jnp.tile(b_s[:, :length], (1, reps))
    return ws, jnp.broadcast_to(bs[:, :, None], (GM_GROUPS, GM_CHUNK, LANE))


def kernel(x_prompt, x_sample, cache_kv_latent, cache_k_rope, p_prompt, p_sample, ffn1_norm, ffn1_w_gate, ffn1_w_up, ffn1_w_down, mix_norm, w_in, q_a_norm, w_uq, q_nope_norm, q_rope_norm, kv_a_norm, k_rope_norm, w_uk, w_uv, k_nope_norm, gm_v_norm, gm_w_s, gm_b_s, w_o, ffn2_norm, ffn2_w_gate, ffn2_w_up, ffn2_w_down, ple_norm, ple_w_gate, ple_w_proj):
    w = dict(ffn1_norm=ffn1_norm, ffn1_w_gate=ffn1_w_gate, ffn1_w_up=ffn1_w_up,
             ffn1_w_down=ffn1_w_down, mix_norm=mix_norm, w_in=w_in, q_a_norm=q_a_norm,
             w_uq=w_uq, q_nope_norm=q_nope_norm, q_rope_norm=q_rope_norm,
             kv_a_norm=kv_a_norm, k_rope_norm=k_rope_norm, w_uk=w_uk, w_uv=w_uv,
             k_nope_norm=k_nope_norm, gm_v_norm=gm_v_norm, w_o=w_o, ffn2_norm=ffn2_norm,
             ffn2_w_gate=ffn2_w_gate, ffn2_w_up=ffn2_w_up, ffn2_w_down=ffn2_w_down,
             ple_norm=ple_norm, ple_w_gate=ple_w_gate, ple_w_proj=ple_w_proj)
    batch, seq, _ = x_prompt.shape
    nb, dec, _ = x_sample.shape
    past = cache_kv_latent.shape[2]
    rows_s = nb * dec
    assert batch == 1 and seq % ATT_TK == 0 and rows_s % ROW_TILE == 0
    assert GM_CHUNK % dec == 0 and past % SAMPLE_TK == 0

    wp = _prep_weights(w)
    tabt_p = _rope_table(jnp.arange(seq, dtype=jnp.int32))
    tabt_s = jnp.tile(_rope_table(past + jnp.arange(dec, dtype=jnp.int32)), (1, nb))
    hp = x_prompt.reshape(seq, D_MODEL)
    hs = x_sample.reshape(rows_s, D_MODEL)
    pp = p_prompt.reshape(DEPTH, seq, PLE_DIM)
    ps = p_sample.reshape(DEPTH, rows_s, PLE_DIM)
    cache_krt = jnp.swapaxes(cache_k_rope, 2, 3)

    outs = {k: [] for k in ("pc", "pkr", "sc", "skr", "sv")}
    for l in range(DEPTH):
        ws_p, bs_p = _gate_operands(gm_w_s[l], gm_b_s[l], GM_CHUNK)
        ws_s, bs_s = _gate_operands(gm_w_s[l], gm_b_s[l], dec)

        hp = _ffn(l, hp, wp, FFN_TILE)
        qt, k, c, krt, sa, gb, vt = _mix_in(l, hp, tabt_p, wp, ws_p, bs_p,
                                            ROW_TILE, True, False)
        oa = _prompt_attention(qt, k, vt)
        hp = _post(l, hp, sa, oa, gb, pp, wp, FFN_TILE)
        outs["pc"].append(c.reshape(batch, seq, KV_LORA))
        outs["pkr"].append(krt.reshape(batch, QK_ROPE, seq))

        hs = _ffn(l, hs, wp, ROW_TILE)
        qt, k, c, krt, sa, gb, vn = _mix_in(l, hs, tabt_s, wp, ws_s, bs_s,
                                            ROW_TILE, False, True)
        q = qt.reshape(N_HEADS, HEAD_PAD, rows_s).transpose(0, 2, 1)
        oa = _sample_attention(l, q, k, c, cache_kv_latent, cache_krt, wp, dec)
        hs = _post(l, hs, sa, oa, gb, ps, wp, ROW_TILE)
        outs["sc"].append(c.reshape(nb, dec, KV_LORA))
        outs["skr"].append(krt.T.reshape(nb, dec, QK_ROPE))
        outs["sv"].append(vn.reshape(nb, dec, GM_WIDTH))

    return (hp.reshape(batch, seq, D_MODEL), hs.reshape(nb, dec, D_MODEL),
            jnp.stack(outs["pc"]), jnp.swapaxes(jnp.stack(outs["pkr"]), 2, 3),
            jnp.stack(outs["sc"]), jnp.stack(outs["skr"]), jnp.stack(outs["sv"]))
```

```python
import functools
import math

import jax
import jax.numpy as jnp
from jax import lax
from jax.experimental import pallas as pl
from jax.experimental.pallas import tpu as pltpu

D_MODEL = 1024
DEPTH = 2
CHUNK = 64
N_HEADS = 8
QK_NOPE = 128
QK_ROPE = 64
V_DIM = 128
Q_LORA = 384
KV_LORA = 512
GM_CHUNK = 128
GM_WIDTH = 1024
GM_GROUPS = 8
D_FF = 2816
PLE_DIM = 256
ROPE_THETA = 10000.0
EPS = 1e-6
SCALE = (QK_NOPE + QK_ROPE) ** -0.5
Q_SCALE = SCALE * math.log2(math.e)

LANE = 128
HEAD_PAD = 2 * LANE
HALF = QK_ROPE // 2
_C0 = Q_LORA + LANE
IN_COLS = dict(qkr=(0, _C0), kv=(_C0, _C0 + KV_LORA),
               u=(_C0 + KV_LORA, _C0 + KV_LORA + GM_WIDTH),
               v=(_C0 + KV_LORA + GM_WIDTH, _C0 + KV_LORA + 2 * GM_WIDTH),
               ga=(_C0 + KV_LORA + 2 * GM_WIDTH, _C0 + KV_LORA + 2 * GM_WIDTH + D_MODEL),
               gb=(_C0 + KV_LORA + 2 * GM_WIDTH + D_MODEL,
                   _C0 + KV_LORA + 2 * GM_WIDTH + 2 * D_MODEL))
VMEM_LIMIT = 56 * 1024 * 1024
FF_CHUNK = 256
ROW_TILE = 256
FFN_TILE = 512
ATT_TQ = 1024
ATT_SUB = 256
ATT_TK = 1024
ATT_HEADS = 2
ATT_UNROLL = 2
ATT_MAX_JUMP = 64.0
SAMPLE_TK = 512
NEG = -1e30

F32 = jnp.float32
BF16 = jnp.bfloat16


def _dot(a, b):
    return jnp.dot(a, b, preferred_element_type=F32)


def _dot_nt(a, b):
    return lax.dot_general(a, b, (((1,), (1,)), ((), ())), preferred_element_type=F32)


def _rms(x, g, n=None):
    n = x.shape[-1] if n is None else n
    ms = jnp.sum(x * x, axis=-1, keepdims=True) * (1.0 / n)
    return x * lax.rsqrt(ms + EPS) * g


def _const_spec(shape):
    nd = len(shape)
    return pl.BlockSpec(shape, lambda *_: (0,) * nd, pipeline_mode=pl.Buffered(1))


def _layer_spec(arr, layer):
    nd = arr.ndim - 1
    return pl.BlockSpec((None,) + arr.shape[1:], lambda *_: (layer,) + (0,) * nd,
                        pipeline_mode=pl.Buffered(1))


def _params(*sem):
    return pltpu.CompilerParams(dimension_semantics=sem, vmem_limit_bytes=VMEM_LIMIT)


def _swiglu_half(x, g, wg_ref, wu_ref, wd_ref):
    n = _rms(x, g).astype(BF16)
    acc = jnp.zeros(x.shape, F32)
    for c in range(D_FF // FF_CHUNK):
        sl = slice(c * FF_CHUNK, (c + 1) * FF_CHUNK)
        gt = _dot(n, wg_ref[:, sl])
        up = _dot(n, wu_ref[:, sl])
        a = (gt * jax.nn.sigmoid(gt) * up).astype(BF16)
        acc = acc + _dot(a, wd_ref[sl, :])
    return x + 0.5 * acc


def _ffn_kernel(xp_ref, xs_ref, g_ref, wg_ref, wu_ref, wd_ref, op_ref, os_ref, *, n_prompt):
    i = pl.program_id(0)

    @pl.when(i < n_prompt)
    def _():
        op_ref[...] = _swiglu_half(xp_ref[...], g_ref[...], wg_ref, wu_ref, wd_ref)

    @pl.when(i == n_prompt)
    def _():
        os_ref[...] = _swiglu_half(xs_ref[...], g_ref[...], wg_ref, wu_ref, wd_ref)


def _ffn(layer, xp, xs, wp, tm):
    n_prompt = xp.shape[0] // tm
    last = n_prompt - 1
    prompt = pl.BlockSpec((tm, D_MODEL), lambda i: (jnp.minimum(i, last), 0))
    sample = pl.BlockSpec(xs.shape, lambda i: (0, 0))
    consts = [wp["g_f1"], wp["f1_wg"], wp["f1_wu"], wp["f1_wd"]]
    return pl.pallas_call(
        functools.partial(_ffn_kernel, n_prompt=n_prompt),
        grid=(n_prompt + 1,),
        in_specs=[prompt, sample] + [_layer_spec(a, layer) for a in consts],
        out_specs=[prompt, sample],
        out_shape=[jax.ShapeDtypeStruct(xp.shape, F32), jax.ShapeDtypeStruct(xs.shape, F32)],
        compiler_params=_params("arbitrary"),
        name="ffn",
    )(xp, xs, *consts)


def _rope_rows(x, tabt):
    t = tabt.T
    lane = lax.broadcasted_iota(jnp.int32, x.shape, 1)
    lo, mid = lane < HALF, lane < QK_ROPE
    to_hi, to_lo = (lambda a: pltpu.roll(a, HALF, 1)), (lambda a: pltpu.roll(a, LANE - HALF, 1))
    cos2 = jnp.where(lo, t, jnp.where(mid, to_hi(t), 0.0))
    sin2 = jnp.where(lo, -to_lo(t), jnp.where(mid, t, 0.0))
    swapped = jnp.where(lo, to_lo(x), to_hi(x))
    return x * cos2 + swapped * sin2


def _mix_kernel(h_ref, tabt_ref, g_qt, ws_ref, bs_ref,
                g_mix, w_all, g_qa, w_uqt, g_kva, g_kr, w_uk, w_uvt, g_kn, g_v,
                *out_refs, tm, emit_v, emit_vn):
    out_refs = list(out_refs)
    qt_ref, k_ref, c_ref, krt_ref, sa_ref, gb_ref = out_refs[:6]
    rest = out_refs[6:]
    vt_ref = rest.pop(0) if emit_v else None
    vn_ref = rest.pop(0) if emit_vn else None

    n = _rms(h_ref[...], g_mix[...]).astype(BF16)

    w_cols = lambda name: w_all[:, IN_COLS[name][0]:IN_COLS[name][1]]
    kv_pre = _dot(n, w_cols("kv"))
    qkr = _dot(n, w_cols("qkr"))
    q_pre, kr_pre = qkr[:, :Q_LORA], qkr[:, Q_LORA:]
    u = _dot(n, w_cols("u"))

    c = _rms(kv_pre, g_kva[...])
    c_ref[...] = c
    cb = c.astype(BF16)
    kn = _dot(cb, w_uk[...])
    gbz = jax.nn.sigmoid(_dot(n, w_cols("gb")))

    kr = _rope_rows(_rms(kr_pre, g_kr[...], QK_ROPE), tabt_ref[...])
    krt_ref[...] = kr.T[:QK_ROPE]
    for h in range(N_HEADS):
        kh = _rms(kn[:, h * QK_NOPE:(h + 1) * QK_NOPE], g_kn[...])
        k_ref[h] = jnp.concatenate([kh, kr], axis=1).astype(BF16)

    ql = _rms(q_pre, g_qa[...]).astype(BF16)
    qft = _dot_nt(w_uqt[...], ql)
    if emit_v:
        vt = _dot_nt(w_uvt[...], cb)
        for h in range(N_HEADS):
            vt_ref[h, 0] = vt[h * V_DIM:(h + 1) * V_DIM].astype(BF16)
    v_pre = _dot(n, w_cols("v"))
    sa_ref[...] = jax.nn.sigmoid(_dot(n, w_cols("ga")))
    cos, sin = tabt_ref[:HALF, :], tabt_ref[HALF:QK_ROPE, :]
    g_nope = g_qt[:QK_NOPE, :]
    g_r1 = g_qt[QK_NOPE:QK_NOPE + HALF, :]
    g_r2 = g_qt[QK_NOPE + HALF:, :]
    zpad = jnp.zeros((HEAD_PAD - QK_NOPE - QK_ROPE, tm), F32)
    for h in range(N_HEADS):
        blk = qft[h * HEAD_PAD:(h + 1) * HEAD_PAD]
        nope = blk[:QK_NOPE]
        x1 = blk[QK_NOPE:QK_NOPE + HALF]
        x2 = blk[QK_NOPE + HALF:QK_NOPE + QK_ROPE]
        rn = lax.rsqrt(jnp.sum(nope * nope, axis=0, keepdims=True) * (1.0 / QK_NOPE) + EPS)
        rr = lax.rsqrt((jnp.sum(x1 * x1, axis=0, keepdims=True)
                        + jnp.sum(x2 * x2, axis=0, keepdims=True)) * (1.0 / QK_ROPE) + EPS)
        y1 = x1 * rr * g_r1
        y2 = x2 * rr * g_r2
        qh = jnp.concatenate([nope * rn * g_nope, y1 * cos - y2 * sin,
                              y1 * sin + y2 * cos, zpad], axis=0)
        qt_ref[h * HEAD_PAD:(h + 1) * HEAD_PAD, :] = (qh * Q_SCALE).astype(BF16)

    vn = _rms(v_pre, g_v[...])
    if emit_vn:
        vn_ref[...] = vn
    vb = vn.astype(BF16)
    nc = tm // GM_CHUNK
    tril = (lax.broadcasted_iota(jnp.int32, (GM_CHUNK, GM_CHUNK), 0)
            >= lax.broadcasted_iota(jnp.int32, (GM_CHUNK, GM_CHUNK), 1))
    for g in range(GM_GROUPS):
        cs = slice(g * LANE, (g + 1) * LANE)
        w = jnp.where(tril, ws_ref[g], 0.0).astype(BF16)
        rhs = jnp.concatenate(
            [vb[c_ * GM_CHUNK:(c_ + 1) * GM_CHUNK, cs] for c_ in range(nc)], axis=1)
        mixed = _dot(w, rhs)
        for c_ in range(nc):
            rs = slice(c_ * GM_CHUNK, (c_ + 1) * GM_CHUNK)
            ob = u[rs, cs] * (mixed[:, c_ * LANE:(c_ + 1) * LANE] + bs_ref[g])
            gb_ref[rs, cs] = gbz[rs, cs] * ob


def _mix_in(layer, h, tabt, wp, ws_eff, bs_eff, tm, emit_v, emit_vn):
    rows = h.shape[0]
    row = lambda w: pl.BlockSpec((tm, w), lambda i: (i, 0))
    col = lambda r: pl.BlockSpec((r, tm), lambda i: (0, i))
    head = lambda w: pl.BlockSpec((N_HEADS, tm, w), lambda i: (0, i, 0))
    g_qt = jnp.broadcast_to(wp["g_qt"][layer], (QK_NOPE + QK_ROPE, tm))
    consts = [g_qt, ws_eff, bs_eff]
    stacked = [wp[k] for k in ("g_mix", "w_all", "g_qa", "w_uqt", "g_kva", "g_kr", "w_uk",
                               "w_uvt", "g_kn", "g_v")]
    out_shape = [jax.ShapeDtypeStruct((N_HEADS * HEAD_PAD, rows), BF16),
                 jax.ShapeDtypeStruct((N_HEADS, rows, HEAD_PAD), BF16),
                 jax.ShapeDtypeStruct((rows, KV_LORA), F32),
                 jax.ShapeDtypeStruct((QK_ROPE, rows), F32),
                 jax.ShapeDtypeStruct((rows, D_MODEL), F32),
                 jax.ShapeDtypeStruct((rows, D_MODEL), F32)]
    out_specs = [col(N_HEADS * HEAD_PAD), head(HEAD_PAD), row(KV_LORA), col(QK_ROPE),
                 row(D_MODEL), row(D_MODEL)]
    if emit_v:
        per = ATT_TK // tm
        out_shape.append(jax.ShapeDtypeStruct((N_HEADS, rows // ATT_TK, V_DIM, ATT_TK), BF16))
        out_specs.append(pl.BlockSpec((N_HEADS, 1, V_DIM, tm),
                                      lambda i: (0, i // per, 0, i % per)))
    if emit_vn:
        out_shape.append(jax.ShapeDtypeStruct((rows, GM_WIDTH), F32))
        out_specs.append(row(GM_WIDTH))
    return pl.pallas_call(
        functools.partial(_mix_kernel, tm=tm, emit_v=emit_v, emit_vn=emit_vn),
        grid=(rows // tm,),
        in_specs=[row(D_MODEL), col(LANE)]
        + [_const_spec(a.shape) for a in consts]
        + [_layer_spec(a, layer) for a in stacked],
        out_specs=out_specs,
        out_shape=out_shape,
        compiler_params=_params("parallel"),
        name="mix_in",
    )(h, tabt, *consts, *stacked)


def _attn_kernel(qt_ref, k_ref, vt_ref, o_ref):
    tq, tk, nh = ATT_TQ, ATT_TK, ATT_HEADS
    i = pl.program_id(1)
    ns, ws = tq // ATT_SUB, ATT_SUB
    chains = [(h, a) for h in range(nh) for a in range(ns)]
    qts = {(h, a): qt_ref[h * HEAD_PAD:(h + 1) * HEAD_PAD, a * ws:(a + 1) * ws]
           for h, a in chains}

    def scores(j, ch):
        start = pl.multiple_of(j * tk, tk)
        return _dot(k_ref[ch[0], pl.ds(start, tk), :], qts[ch])

    n_full = (i * tq) // tk
    dk = (lax.broadcasted_iota(jnp.int32, (tk, ws), 0) // CHUNK
          - lax.broadcasted_iota(jnp.int32, (tk, ws), 1) // CHUNK)
    off = (i * tq - n_full * tk) // CHUNK
    masks = [dk <= off + a * (ws // CHUNK) for a in range(ns)]

    def write(ch, l, acc):
        h, a = ch
        o_ref[a * ws:(a + 1) * ws, h * V_DIM:(h + 1) * V_DIM] = (acc / l).T

    def update(st, m, l, jump):
        p = jnp.exp2(st - m)
        m_new = jnp.maximum(m, jnp.max(st, axis=0, keepdims=True))
        alpha = jnp.exp2(m - m_new)
        l = alpha * (l + jnp.sum(p, axis=0, keepdims=True))
        return m_new, l, jnp.maximum(jump, m_new - m), alpha, p.astype(BF16)

    def body1(j, states):
        sts = [scores(j, ch) for ch in chains]
        ups = [update(st, m, l, jump) for st, (m, l, _, jump) in zip(sts, states)]
        new_states = []
        for ch, state, (m, l, jump, alpha, p) in zip(chains, states, ups):
            acc = alpha * (state[2] + _dot(vt_ref[ch[0], j], p))
            new_states.append((m, l, acc, jump))
        return tuple(new_states)

    def body_n(jj, states):
        js = [ATT_UNROLL * jj + t for t in range(ATT_UNROLL)]
        sts = [[scores(j, ch) for ch in chains] for j in js]
        ups, prev = [], [(m, l, jump) for (m, l, _, jump) in states]
        for t in range(ATT_UNROLL):
            up = [update(st, *pv) for st, pv in zip(sts[t], prev)]
            ups.append(up)
            prev = [(u[0], u[1], u[2]) for u in up]
        pvs = [[_dot(vt_ref[ch[0], js[t]], up[4]) for ch, up in zip(chains, ups[t])]
               for t in range(ATT_UNROLL)]
        new_states = []
        for c_, state in enumerate(states):
            acc = state[2]
            for t in range(ATT_UNROLL):
                acc = ups[t][c_][3] * (acc + pvs[t][c_])
            last = ups[-1][c_]
            new_states.append((last[0], last[1], acc, last[2]))
        return tuple(new_states)

    init = tuple(
        (jnp.max(_dot(k_ref[ch[0], 0:CHUNK, :], qts[ch]), axis=0, keepdims=True),
         jnp.zeros((1, ws), F32), jnp.zeros((V_DIM, ws), F32), jnp.zeros((1, ws), F32))
        for ch in chains)
    n_trips = n_full // ATT_UNROLL
    states = lax.fori_loop(0, n_trips, body_n, init)
    states = lax.fori_loop(n_trips * ATT_UNROLL, n_full, body1, states)
    worst = jnp.zeros((1, ws), F32)
    sts = [jnp.where(masks[ch[1]], scores(n_full, ch), NEG) for ch in chains]
    ups = [update(st, m, l, jump) for st, (m, l, _, jump) in zip(sts, states)]
    for ch, state, (_, l, jump, alpha, p) in zip(chains, states, ups):
        write(ch, l, alpha * (state[2] + _dot(vt_ref[ch[0], n_full], p)))
        worst = jnp.maximum(worst, jump)

    @pl.when(jnp.max(worst) > ATT_MAX_JUMP)
    def _():
        def exact(st, state):
            m, l, acc = state
            m_new = jnp.maximum(m, jnp.max(st, axis=0, keepdims=True))
            alpha = jnp.exp2(m - m_new)
            p = jnp.exp2(st - m_new)
            return m_new, alpha * l + jnp.sum(p, axis=0, keepdims=True), alpha * acc, p

        for ch in chains:
            def ebody(j, state, ch=ch):
                m, l, acc, p = exact(scores(j, ch), state)
                return m, l, acc + _dot(vt_ref[ch[0], j], p.astype(BF16))

            state = lax.fori_loop(0, n_full, ebody, (jnp.full((1, ws), NEG, F32),
                                                    jnp.zeros((1, ws), F32),
                                                    jnp.zeros((V_DIM, ws), F32)))
            st = jnp.where(masks[ch[1]], scores(n_full, ch), NEG)
            _, l, acc, p = exact(st, state)
            write(ch, l, acc + _dot(vt_ref[ch[0], n_full], p.astype(BF16)))


def _prompt_attention(qt, k, vt):
    s = k.shape[1]
    nh = ATT_HEADS
    once = pl.Buffered(1)
    return pl.pallas_call(
        _attn_kernel,
        grid=(N_HEADS // nh, s // ATT_TQ),
        in_specs=[pl.BlockSpec((nh * HEAD_PAD, ATT_TQ), lambda h, i: (h, i)),
                  pl.BlockSpec((nh, s, HEAD_PAD), lambda h, i: (h, 0, 0), pipeline_mode=once),
                  pl.BlockSpec((nh, s // ATT_TK, V_DIM, ATT_TK), lambda h, i: (h, 0, 0, 0),
                               pipeline_mode=once)],
        out_specs=pl.BlockSpec((ATT_TQ, nh * V_DIM), lambda h, i: (i, h)),
        out_shape=jax.ShapeDtypeStruct((s, N_HEADS * V_DIM), F32),
        compiler_params=_params("parallel", "arbitrary"),
        name="prompt_attn",
    )(qt, k, vt)


def _sample_attn_kernel(q_ref, kn_ref, cn_ref, cc_ref, ckrt_ref, w_ukt, w_uv, g_kn,
                        o_ref, s_scr, cb_scr, *, dec, past):
    tkc = SAMPLE_TK
    nch = past // tkc
    qg = [(q_ref[h][:, :QK_NOPE].astype(F32) * g_kn[...]).astype(BF16) for h in range(N_HEADS)]
    qr_all = jnp.concatenate(
        [q_ref[h][:, QK_NOPE:QK_NOPE + QK_ROPE] for h in range(N_HEADS)], axis=0)

    for j in range(nch):
        cb = cc_ref[0, 0, j * tkc:(j + 1) * tkc, :].astype(BF16)
        cb_scr[j] = cb
        knt = _dot_nt(w_ukt[...], cb)
        s_rope = _dot(qr_all, ckrt_ref[0, 0, :, j * tkc:(j + 1) * tkc].astype(BF16))
        for h in range(N_HEADS):
            kh = knt[h * QK_NOPE:(h + 1) * QK_NOPE]
            r = lax.rsqrt(jnp.sum(kh * kh, axis=0, keepdims=True) * (1.0 / QK_NOPE) + EPS)
            s_scr[j, h * dec:(h + 1) * dec, :] = (
                _dot(qg[h], kh.astype(BF16)) * r + s_rope[h * dec:(h + 1) * dec])

    s_new = jnp.concatenate([_dot_nt(q_ref[h], kn_ref[h]) for h in range(N_HEADS)], axis=0)
    m = jnp.max(s_new, axis=-1, keepdims=True)
    for j in range(nch):
        m = jnp.maximum(m, jnp.max(s_scr[j], axis=-1, keepdims=True))
    p_n = jnp.exp2(s_new - m)
    l = jnp.sum(p_n, axis=-1, keepdims=True)
    pc = _dot(p_n.astype(BF16), cn_ref[...].astype(BF16))
    for j in range(nch):
        p_c = jnp.exp2(s_scr[j] - m)
        l = l + jnp.sum(p_c, axis=-1, keepdims=True)
        pc = pc + _dot(p_c.astype(BF16), cb_scr[j])
    pcb = (pc / l).astype(BF16)
    for h in range(N_HEADS):
        o_ref[:, h * V_DIM:(h + 1) * V_DIM] = _dot(
            pcb[h * dec:(h + 1) * dec], w_uv[:, h * V_DIM:(h + 1) * V_DIM])


def _sample_attention(layer, q, k_new, c_new, cache_c, cache_krt, wp, dec):
    _, nb, past, _ = cache_c.shape
    rows = nb * dec
    consts = [wp["w_ukt"], wp["w_uv"], wp["g_kn"]]
    return pl.pallas_call(
        functools.partial(_sample_attn_kernel, dec=dec, past=past),
        grid=(nb,),
        in_specs=[pl.BlockSpec((N_HEADS, dec, HEAD_PAD), lambda b: (0, b, 0)),
                  pl.BlockSpec((N_HEADS, dec, HEAD_PAD), lambda b: (0, b, 0)),
                  pl.BlockSpec((dec, KV_LORA), lambda b: (b, 0)),
                  pl.BlockSpec((1, 1, past, KV_LORA), lambda b: (layer, b, 0, 0)),
                  pl.BlockSpec((1, 1, QK_ROPE, past), lambda b: (layer, b, 0, 0))]
        + [_layer_spec(a, layer) for a in consts],
        out_specs=pl.BlockSpec((dec, N_HEADS * V_DIM), lambda b: (b, 0)),
        out_shape=jax.ShapeDtypeStruct((rows, N_HEADS * V_DIM), F32),
        scratch_shapes=[pltpu.VMEM((past // SAMPLE_TK, N_HEADS * dec, SAMPLE_TK), F32),
                        pltpu.VMEM((past // SAMPLE_TK, SAMPLE_TK, KV_LORA), BF16)],
        compiler_params=_params("parallel"),
        name="sample_attn",
    )(q, k_new, c_new, cache_c, cache_krt, *consts)


def _post_kernel(h_ref, sa_ref, oa_ref, gb_ref, p_ref, w_o, g_f2, wg, wu, wd,
                 g_ple, w_pg, w_pe, o_ref):
    mixed = (sa_ref[...] * oa_ref[...] + gb_ref[...]).astype(BF16)
    h = h_ref[...] + _dot(mixed, w_o[...])
    h = _swiglu_half(h, g_f2[...], wg, wu, wd)
    gate = jax.nn.sigmoid(_dot(_rms(h, g_ple[...]).astype(BF16), w_pg[...]))
    o_ref[...] = h + gate * _dot(p_ref[0].astype(BF16), w_pe[...])


def _post(layer, h, sa, oa, gb, p, wp, tm):
    rows = h.shape[0]
    row = lambda w: pl.BlockSpec((tm, w), lambda i: (i, 0))
    consts = [wp[k] for k in ("w_o", "g_f2", "f2_wg", "f2_wu", "f2_wd", "g_ple", "w_pg", "w_pe")]
    return pl.pallas_call(
        _post_kernel,
        grid=(rows // tm,),
        in_specs=[row(D_MODEL)] * 4
        + [pl.BlockSpec((1, tm, PLE_DIM), lambda i: (layer, i, 0))]
        + [_layer_spec(a, layer) for a in consts],
        out_specs=row(D_MODEL),
        out_shape=jax.ShapeDtypeStruct(h.shape, F32),
        compiler_params=_params("parallel"),
        name="post",
    )(h, sa, oa, gb, p, *consts)


def _rope_table(pos):
    inv = ROPE_THETA ** (-jnp.arange(HALF, dtype=F32) / HALF)
    ang_t = inv[:, None] * pos.astype(F32)[None, :]
    pad = jnp.zeros((LANE - QK_ROPE, pos.shape[0]), F32)
    return jnp.concatenate([jnp.cos(ang_t), jnp.sin(ang_t), pad], axis=0)


def _prep_weights(w):
    row = lambda a: a[:, None, :].astype(F32)
    pad_lane = lambda a: jnp.pad(a, [(0, 0)] * (a.ndim - 1) + [(0, LANE - a.shape[-1])])
    b = lambda a: a.astype(BF16)
    w_in = w["w_in"]
    o_kv = Q_LORA
    o_kr = o_kv + KV_LORA
    o_u = o_kr + QK_ROPE
    w_uq = w["w_uq"].reshape(DEPTH, Q_LORA, N_HEADS, QK_NOPE + QK_ROPE)
    w_uq = jnp.pad(w_uq, ((0, 0), (0, 0), (0, 0), (0, HEAD_PAD - QK_NOPE - QK_ROPE)))
    w_uq = w_uq.reshape(DEPTH, Q_LORA, N_HEADS * HEAD_PAD)
    w_uv = w["w_uv"].reshape(DEPTH, KV_LORA, N_HEADS * V_DIM)
    w_uk = w["w_uk"].reshape(DEPTH, KV_LORA, N_HEADS * QK_NOPE)
    g_qt = jnp.concatenate([w["q_nope_norm"], w["q_rope_norm"]], axis=1)[:, :, None]
    return dict(
        g_f1=row(w["ffn1_norm"]), f1_wg=b(w["ffn1_w_gate"]), f1_wu=b(w["ffn1_w_up"]),
        f1_wd=b(w["ffn1_w_down"]),
        g_mix=row(w["mix_norm"]),
        w_all=b(jnp.concatenate(
            [w_in[:, :, :o_kv], pad_lane(w_in[:, :, o_kr:o_u]), w_in[:, :, o_kv:o_kr],
             w_in[:, :, o_u:]], axis=-1)),
        g_qa=row(w["q_a_norm"]), w_uqt=b(jnp.swapaxes(w_uq, 1, 2)), g_qt=g_qt.astype(F32),
        g_kva=row(w["kv_a_norm"]), g_kr=pad_lane(row(w["k_rope_norm"])),
        w_uk=b(w_uk), w_ukt=b(jnp.swapaxes(w_uk, 1, 2)),
        w_uv=b(w_uv), w_uvt=b(jnp.swapaxes(w_uv, 1, 2)),
        g_kn=row(w["k_nope_norm"]), g_v=row(w["gm_v_norm"]),
        w_o=b(w["w_o"]),
        g_f2=row(w["ffn2_norm"]), f2_wg=b(w["ffn2_w_gate"]), f2_wu=b(w["ffn2_w_up"]),
        f2_wd=b(w["ffn2_w_down"]),
        g_ple=row(w["ple_norm"]), w_pg=b(w["ple_w_gate"]), w_pe=b(w["ple_w_proj"]),
    )


def _gate_operands(w_s, b_s, length):
    reps = GM_CHUNK // length
    eye = jnp.eye(reps, dtype=w_s.dtype)
    ws = jnp.einsum("ab,gts->gatbs", eye, w_s[:, :length, :length])
    ws = ws.reshape(GM_GROUPS, GM_CHUNK, GM_CHUNK)
    bs = jnp.tile(b_s[:, :length], (1, reps))
    return ws, jnp.broadcast_to(bs[:, :, None], (GM_GROUPS, GM_CHUNK, LANE))


def kernel(x_prompt, x_sample, cache_kv_latent, cache_k_rope, p_prompt, p_sample, ffn1_norm, ffn1_w_gate, ffn1_w_up, ffn1_w_down, mix_norm, w_in, q_a_norm, w_uq, q_nope_norm, q_rope_norm, kv_a_norm, k_rope_norm, w_uk, w_uv, k_nope_norm, gm_v_norm, gm_w_s, gm_b_s, w_o, ffn2_norm, ffn2_w_gate, ffn2_w_up, ffn2_w_down, ple_norm, ple_w_gate, ple_w_proj):
    w = dict(ffn1_norm=ffn1_norm, ffn1_w_gate=ffn1_w_gate, ffn1_w_up=ffn1_w_up,
             ffn1_w_down=ffn1_w_down, mix_norm=mix_norm, w_in=w_in, q_a_norm=q_a_norm,
             w_uq=w_uq, q_nope_norm=q_nope_norm, q_rope_norm=q_rope_norm,
             kv_a_norm=kv_a_norm, k_rope_norm=k_rope_norm, w_uk=w_uk, w_uv=w_uv,
             k_nope_norm=k_nope_norm, gm_v_norm=gm_v_norm, w_o=w_o, ffn2_norm=ffn2_norm,
             ffn2_w_gate=ffn2_w_gate, ffn2_w_up=ffn2_w_up, ffn2_w_down=ffn2_w_down,
             ple_norm=ple_norm, ple_w_gate=ple_w_gate, ple_w_proj=ple_w_proj)
    batch, seq, _ = x_prompt.shape
    nb, dec, _ = x_sample.shape
    past = cache_kv_latent.shape[2]
    rows_s = nb * dec
    assert batch == 1 and seq % ATT_TK == 0 and rows_s % ROW_TILE == 0
    assert GM_CHUNK % dec == 0 and past % SAMPLE_TK == 0

    wp = _prep_weights(w)
    tabt_p = _rope_table(jnp.arange(seq, dtype=jnp.int32))
    tabt_s = jnp.tile(_rope_table(past + jnp.arange(dec, dtype=jnp.int32)), (1, nb))
    hp = x_prompt.reshape(seq, D_MODEL)
    hs = x_sample.reshape(rows_s, D_MODEL)
    pp = p_prompt.reshape(DEPTH, seq, PLE_DIM)
    ps = p_sample.reshape(DEPTH, rows_s, PLE_DIM)
    cache_krt = jnp.swapaxes(cache_k_rope, 2, 3)

    outs = {k: [] for k in ("pc", "pkr", "sc", "skr", "sv")}
    for l in range(DEPTH):
        ws_p, bs_p = _gate_operands(gm_w_s[l], gm_b_s[l], GM_CHUNK)
        ws_s, bs_s = _gate_operands(gm_w_s[l], gm_b_s[l], dec)

        hp, hs = _ffn(l, hp, hs, wp, FFN_TILE)

        qt, k, c, krt, sa, gb, vt = _mix_in(l, hp, tabt_p, wp, ws_p, bs_p,
                                            ROW_TILE, True, False)
        oa = _prompt_attention(qt, k, vt)
        hp = _post(l, hp, sa, oa, gb, pp, wp, FFN_TILE)
        outs["pc"].append(c.reshape(batch, seq, KV_LORA))
        outs["pkr"].append(krt.reshape(batch, QK_ROPE, seq))

        qt, k, c, krt, sa, gb, vn = _mix_in(l, hs, tabt_s, wp, ws_s, bs_s,
                                            ROW_TILE, False, True)
        q = qt.reshape(N_HEADS, HEAD_PAD, rows_s).transpose(0, 2, 1)
        oa = _sample_attention(l, q, k, c, cache_kv_latent, cache_krt, wp, dec)
        hs = _post(l, hs, sa, oa, gb, ps, wp, ROW_TILE)
        outs["sc"].append(c.reshape(nb, dec, KV_LORA))
        outs["skr"].append(krt.T.reshape(nb, dec, QK_ROPE))
        outs["sv"].append(vn.reshape(nb, dec, GM_WIDTH))

    return (hp.reshape(batch, seq, D_MODEL), hs.reshape(nb, dec, D_MODEL),
            jnp.stack(outs["pc"]), jnp.swapaxes(jnp.stack(outs["pkr"]), 2, 3),
            jnp.stack(outs["sc"]), jnp.stack(outs["skr"]), jnp.stack(outs["sv"]))
```
